```python
import math
import jax, jax.numpy as jnp
from jax import lax
import numpy as np

D_MODEL = 1024
BATCH = 4
SEQ = 4096
DEPTH = 4
DEC_BATCH = 128
DEC_SEQ = 8
PAST_LEN = 8192
PAGE_SIZE = 128

NUM_BUCKETS = 32
MAX_DISTANCE = 2048
N_ATT_HEADS = 12
Q_BLOCK = 128
RMS_EPS = 1e-6
NEG_INF = -1e30
D_FF = 2816

HQ_A = N_ATT_HEADS
HKV_A = 4
DH_A = 64
WIN_A = 128

D_INNER = 2 * D_MODEL
P_B = 64
NH_B = D_INNER // P_B
N_B = 128
NG_B = 4
CONV_W = 4
CONV_DIM = D_INNER + 2 * NG_B * N_B
SSD_CHUNK = 128

C_GROUPS = ((128, 1), (512, 4), (2048, 16))
N_GROUPS_C = len(C_GROUPS)
HPG_C = N_ATT_HEADS // N_GROUPS_C
DH_C = 128

N_EVEN = (DEPTH + 1) // 2
N_ODD = DEPTH // 2

AB_Q = HQ_A * DH_A
AB_KV = HKV_A * DH_A
AB_SPLITS = (AB_Q, AB_Q + AB_KV, AB_Q + 2 * AB_KV, AB_Q + 2 * AB_KV + D_INNER,
             AB_Q + 2 * AB_KV + D_INNER + CONV_DIM)
AB_IN = AB_SPLITS[-1] + NH_B
AB_OUT = AB_Q + D_INNER
C_IN = 3 * N_ATT_HEADS * DH_C
C_OUT = HPG_C * DH_C

kernel_name = 'hybrid_swa_ssd_dilated_macaron_step'


def rmsnorm(x, w):
    xf = x.astype(jnp.float32)
    y = xf * lax.rsqrt(jnp.mean(xf * xf, axis=-1, keepdims=True) + RMS_EPS)
    return (y * w.astype(jnp.float32)).astype(x.dtype)


def swiglu(h, w_gate, w_up, w_down):
    return (jax.nn.silu(h @ w_gate) * (h @ w_up)) @ w_down


def t5_bucket(dist):
    n = jnp.maximum(dist, 0)
    max_exact = NUM_BUCKETS // 2
    nf = jnp.maximum(n, 1).astype(jnp.float32)
    large = max_exact + (jnp.log(nf / max_exact) / math.log(MAX_DISTANCE / max_exact)
                         * (NUM_BUCKETS - max_exact)).astype(jnp.int32)
    large = jnp.minimum(large, NUM_BUCKETS - 1)
    return jnp.where(n < max_exact, n, large)


def softmax_with_sink(logits, sink):
    m = jnp.max(logits, axis=-1, keepdims=True)
    if sink is not None:
        m = jnp.maximum(m, sink)
    p = jnp.exp(logits - m)
    denom = jnp.sum(p, axis=-1, keepdims=True)
    if sink is not None:
        denom = denom + jnp.exp(sink - m)
    return p / denom, (m + jnp.log(denom))[..., 0]


def banded_window_attn(q, k, v, bias_heads, dil, n_win, sinks):
    b, L, hq, dh = q.shape
    hkv = k.shape[2]
    g = hq // hkv
    nb = -(-L // Q_BLOCK)
    pad = nb * Q_BLOCK - L
    padw = ((0, 0), (0, pad), (0, 0), (0, 0))
    q, k, v = (jnp.pad(t, padw) for t in (q, k, v))
    qb = q.reshape(b, nb, Q_BLOCK, hkv, g, dh)
    kb = k.reshape(b, nb, Q_BLOCK, hkv, dh)
    vb = v.reshape(b, nb, Q_BLOCK, hkv, dh)
    prev = ((0, 0), (1, 0), (0, 0), (0, 0), (0, 0))
    kk = jnp.concatenate([jnp.pad(kb[:, :-1], prev), kb], axis=2)
    vv = jnp.concatenate([jnp.pad(vb[:, :-1], prev), vb], axis=2)
    logits = jnp.einsum('bnqkgd,bnskd->bnkgqs', qb, kk).astype(jnp.float32) * (dh ** -0.5)
    qi = jnp.arange(Q_BLOCK)[:, None]
    si = jnp.arange(2 * Q_BLOCK)[None, :]
    dist = qi + Q_BLOCK - si
    key_idx = jnp.arange(nb)[:, None, None] * Q_BLOCK - Q_BLOCK + si
    valid = (dist >= 0) & (dist <= n_win) & (key_idx >= 0)
    bias = bias_heads[t5_bucket(dist * dil)].astype(jnp.float32)
    bias = jnp.transpose(bias, (2, 0, 1)).reshape(hkv, g, Q_BLOCK, 2 * Q_BLOCK)
    logits = jnp.where(valid[None, :, None, None], logits + bias, NEG_INF)
    sink = None if sinks is None else sinks.astype(jnp.float32).reshape(hkv, g, 1, 1)
    probs, lse = softmax_with_sink(logits, sink)
    out = jnp.einsum('bnkgqs,bnskd->bnqkgd', probs.astype(v.dtype), vv)
    out = out.reshape(b, nb * Q_BLOCK, hq, dh)[:, :L]
    lse = jnp.transpose(lse, (0, 1, 4, 2, 3)).reshape(b, nb * Q_BLOCK, hq)[:, :L]
    return out, lse


def dilated_prompt_attn(q, k, v, bias_heads, dil, n_win):
    b, L, h, dh = q.shape

    def fold(t):
        return jnp.transpose(t.reshape(b, L // dil, dil, h, dh), (0, 2, 1, 3, 4)).reshape(b * dil, L // dil, h, dh)

    out, lse = banded_window_attn(fold(q), fold(k), fold(v), bias_heads, dil, n_win, None)
    out = jnp.transpose(out.reshape(b, dil, L // dil, h, dh), (0, 2, 1, 3, 4)).reshape(b, L, h, dh)
    lse = jnp.transpose(lse.reshape(b, dil, L // dil, h), (0, 2, 1, 3)).reshape(b, L, h)
    return out, lse


def gathered_window_attn(q, k_all, v_all, bias_heads, dil, n_win, sinks):
    b, t, hq, dh = q.shape
    l_tot, hkv = k_all.shape[1], k_all.shape[2]
    g = hq // hkv
    dist = jnp.arange(n_win + 1) * dil
    idx = (l_tot - t) + jnp.arange(t)[:, None] - dist[None, :]
    valid = idx >= 0
    idx = jnp.clip(idx, 0, l_tot - 1)
    kg = k_all[:, idx]
    vg = v_all[:, idx]
    qg = q.reshape(b, t, hkv, g, dh)
    logits = jnp.einsum('btkgd,btjkd->bkgtj', qg, kg).astype(jnp.float32) * (dh ** -0.5)
    bias = bias_heads[t5_bucket(dist)].astype(jnp.float32).T.reshape(hkv, g, 1, n_win + 1)
    logits = jnp.where(valid, logits + bias, NEG_INF)
    sink = None if sinks is None else sinks.astype(jnp.float32).reshape(hkv, g, 1, 1)
    probs, lse = softmax_with_sink(logits, sink)
    out = jnp.einsum('bkgtj,btjkd->btkgd', probs.astype(v_all.dtype), vg).reshape(b, t, hq, dh)
    lse = jnp.transpose(lse, (0, 3, 1, 2)).reshape(b, t, hq)
    return out, lse


def causal_dwconv(u, state, w, bias):
    up = jnp.concatenate([state.astype(u.dtype), u], axis=1)
    out = lax.conv_general_dilated(up, w[:, None, :].astype(u.dtype), window_strides=(1,), padding='VALID',
                                   dimension_numbers=('NWC', 'WIO', 'NWC'), feature_group_count=u.shape[-1])
    return out + bias.astype(u.dtype), up[:, -(CONV_W - 1):]


def ssd_scan(x, dt, a, bm, cm, h0):
    f32 = jnp.float32
    b, L, nh, p = x.shape
    ng, n = bm.shape[2], bm.shape[3]
    hg = nh // ng
    cl = min(SSD_CHUNK, L)
    nc = -(-L // cl)
    pad = nc * cl - L
    x = jnp.pad(x.astype(f32), ((0, 0), (0, pad), (0, 0), (0, 0))).reshape(b, nc, cl, ng, hg, p)
    dt = jnp.pad(dt.astype(f32), ((0, 0), (0, pad), (0, 0))).reshape(b, nc, cl, ng, hg)
    bm = jnp.pad(bm.astype(f32), ((0, 0), (0, pad), (0, 0), (0, 0))).reshape(b, nc, cl, ng, n)
    cm = jnp.pad(cm.astype(f32), ((0, 0), (0, pad), (0, 0), (0, 0))).reshape(b, nc, cl, ng, n)
    a_cum = jnp.cumsum(dt * a.reshape(ng, hg), axis=2)
    seg = a_cum[:, :, :, None] - a_cum[:, :, None, :]
    causal = jnp.tril(jnp.ones((cl, cl), dtype=bool))[:, :, None, None]
    decay = jnp.exp(jnp.where(causal, seg, NEG_INF))
    cb = jnp.einsum('bclgn,bcsgn->bclsg', cm, bm)
    y_diag = jnp.einsum('bclsgh,bcsghp->bclghp', cb[..., None] * decay * dt[:, :, None], x)
    decay_end = jnp.exp(a_cum[:, :, -1:] - a_cum) * dt
    chunk_states = jnp.einsum('bclgh,bclghp,bclgn->bcghpn', decay_end, x, bm)
    chunk_decay = jnp.exp(a_cum[:, :, -1])

    def step(h, inp):
        s, dcy = inp
        return dcy[..., None, None] * h + s, h

    h_last, h_starts = lax.scan(step, h0.astype(f32).reshape(b, ng, hg, p, n),
                                (jnp.moveaxis(chunk_states, 1, 0), jnp.moveaxis(chunk_decay, 1, 0)))
    h_starts = jnp.moveaxis(h_starts, 0, 1)
    y_off = jnp.einsum('bclgn,bcghpn->bclghp', cm, h_starts) * jnp.exp(a_cum)[..., None]
    y = (y_diag + y_off).reshape(b, nc * cl, nh, p)[:, :L]
    return y, h_last.reshape(b, nh, p, n)


def gated_group_rmsnorm(y, z, w):
    b, L, d = y.shape
    g = (y.astype(jnp.float32) * jax.nn.silu(z.astype(jnp.float32))).reshape(b, L, NG_B, d // NG_B)
    g = g * lax.rsqrt(jnp.mean(g * g, axis=-1, keepdims=True) + RMS_EPS)
    return (g.reshape(b, L, d) * w.astype(jnp.float32)).astype(z.dtype)


def ab_mixer(h, kv_past, conv_state, ssm_state, w_in, w_out, sinks, conv_w, conv_b, dt_bias, a_log,
             d_skip, norm_w, rel_bias):
    b, L, _ = h.shape
    q, k, v, z, xbc, dt = jnp.split(h @ w_in, AB_SPLITS, axis=-1)
    q = q.reshape(b, L, HQ_A, DH_A)
    k = k.reshape(b, L, HKV_A, DH_A)
    v = v.reshape(b, L, HKV_A, DH_A)
    bias_heads = rel_bias[:, :HQ_A]
    if kv_past is None:
        att, _ = banded_window_attn(q, k, v, bias_heads, 1, WIN_A, sinks)
        k_all, v_all, keep = k, v, min(WIN_A, L)
        conv_state = jnp.zeros((b, CONV_W - 1, CONV_DIM), h.dtype)
        ssm_state = jnp.zeros((b, NH_B, P_B, N_B), jnp.float32)
    else:
        k_all = jnp.concatenate([kv_past[0].astype(k.dtype), k], axis=1)
        v_all = jnp.concatenate([kv_past[1].astype(v.dtype), v], axis=1)
        keep = kv_past[0].shape[1]
        att, _ = gathered_window_attn(q, k_all, v_all, bias_heads, 1, WIN_A, sinks)
    xbc, new_conv = causal_dwconv(xbc, conv_state, conv_w, conv_b)
    xbc = jax.nn.silu(xbc)
    xs, bm, cm = jnp.split(xbc, [D_INNER, D_INNER + NG_B * N_B], axis=-1)
    xs = xs.reshape(b, L, NH_B, P_B)
    dt = jax.nn.softplus(dt.astype(jnp.float32) + dt_bias.astype(jnp.float32))
    a = -jnp.exp(a_log.astype(jnp.float32))
    y, new_ssm = ssd_scan(xs, dt, a, bm.reshape(b, L, NG_B, N_B), cm.reshape(b, L, NG_B, N_B), ssm_state)
    y = y + d_skip.astype(jnp.float32)[:, None] * xs.astype(jnp.float32)
    y = gated_group_rmsnorm(y.reshape(b, L, D_INNER), z, norm_w)
    out = jnp.concatenate([att.reshape(b, L, AB_Q), y.astype(att.dtype)], axis=-1) @ w_out
    return out, (k_all[:, -keep:], v_all[:, -keep:], new_ssm.astype(ssm_state.dtype), new_conv)


def c_mixer(h, kv_past, w_in, w_out, rel_bias):
    b, L, _ = h.shape
    q, k, v = (t.reshape(b, L, N_GROUPS_C, HPG_C, DH_C) for t in jnp.split(h @ w_in, 3, axis=-1))
    outs, lses, new_rows = [], [], []
    for gi, (win, dil) in enumerate(C_GROUPS):
        n_win = win // dil
        qg, kg, vg = q[:, :, gi], k[:, :, gi], v[:, :, gi]
        bias_heads = rel_bias[:, gi * HPG_C:(gi + 1) * HPG_C]
        if kv_past is None:
            o, lse = dilated_prompt_attn(qg, kg, vg, bias_heads, dil, n_win)
            k_all, v_all, keep = kg, vg, min(win, L)
        else:
            k_past, v_past = kv_past[2 * gi], kv_past[2 * gi + 1]
            k_all = jnp.concatenate([k_past.astype(kg.dtype), kg], axis=1)
            v_all = jnp.concatenate([v_past.astype(vg.dtype), vg], axis=1)
            keep = k_past.shape[1]
            o, lse = gathered_window_attn(qg, k_all, v_all, bias_heads, dil, n_win, None)
        outs.append(o)
        lses.append(lse)
        new_rows += [k_all[:, -keep:], v_all[:, -keep:]]
    wts = jax.nn.softmax(jnp.stack(lses, axis=-1), axis=-1)
    o = jnp.einsum('blhgd,blhg->blhd', jnp.stack(outs, axis=3), wts.astype(outs[0].dtype))
    return o.reshape(b, L, C_OUT) @ w_out, tuple(new_rows)


def setup_inputs(seed: int = 0) -> dict:
    key = jax.random.key(seed)
    keys = list(jax.random.split(key, 40))
    f32 = jnp.float32

    def normal(shape, scale):
        return scale * jax.random.normal(keys.pop(), shape, f32)

    def gain(shape):
        return 1.0 + normal(shape, 0.01)

    a_buf = min(WIN_A, PAST_LEN)
    c_buf = [min(w, PAST_LEN) for w, _ in C_GROUPS]
    dt0 = jnp.exp(jax.random.uniform(keys.pop(), (N_EVEN, NH_B), f32, math.log(1e-3), math.log(1e-1)))
    a_init = jax.random.uniform(keys.pop(), (N_EVEN, NH_B), f32, 1.0, 16.0)
    return {
        'x_prompt': normal((BATCH, SEQ, D_MODEL), 1.0),
        'x_sample': normal((DEC_BATCH, DEC_SEQ, D_MODEL), 1.0),
        'cache_a_k': normal((N_EVEN, DEC_BATCH, a_buf, HKV_A, DH_A), 1.0),
        'cache_a_v': normal((N_EVEN, DEC_BATCH, a_buf, HKV_A, DH_A), 1.0),
        'state_b_ssm': normal((N_EVEN, DEC_BATCH, NH_B, P_B, N_B), 0.1),
        'state_b_conv': normal((N_EVEN, DEC_BATCH, CONV_W - 1, CONV_DIM), 1.0),
        'cache_c1_k': normal((N_ODD, DEC_BATCH, c_buf[0], HPG_C, DH_C), 1.0),
        'cache_c1_v': normal((N_ODD, DEC_BATCH, c_buf[0], HPG_C, DH_C), 1.0),
        'cache_c2_k': normal((N_ODD, DEC_BATCH, c_buf[1], HPG_C, DH_C), 1.0),
        'cache_c2_v': normal((N_ODD, DEC_BATCH, c_buf[1], HPG_C, DH_C), 1.0),
        'cache_c3_k': normal((N_ODD, DEC_BATCH, c_buf[2], HPG_C, DH_C), 1.0),
        'cache_c3_v': normal((N_ODD, DEC_BATCH, c_buf[2], HPG_C, DH_C), 1.0),
        'rel_bias': normal((NUM_BUCKETS, N_ATT_HEADS), 0.2),
        'norm_ff1': gain((DEPTH, D_MODEL)),
        'norm_mix': gain((DEPTH, D_MODEL)),
        'norm_ff2': gain((DEPTH, D_MODEL)),
        'norm_final': gain((D_MODEL,)),
        'ff1_gate': normal((DEPTH, D_MODEL, D_FF), D_MODEL ** -0.5),
        'ff1_up': normal((DEPTH, D_MODEL, D_FF), D_MODEL ** -0.5),
        'ff1_down': normal((DEPTH, D_FF, D_MODEL), D_FF ** -0.5),
        'ff2_gate': normal((DEPTH, D_MODEL, D_FF), D_MODEL ** -0.5),
        'ff2_up': normal((DEPTH, D_MODEL, D_FF), D_MODEL ** -0.5),
        'ff2_down': normal((DEPTH, D_FF, D_MODEL), D_FF ** -0.5),
        'ab_w_in': normal((N_EVEN, D_MODEL, AB_IN), D_MODEL ** -0.5),
        'ab_w_out': normal((N_EVEN, AB_OUT, D_MODEL), AB_OUT ** -0.5),
        'a_sinks': normal((N_EVEN, HQ_A), 0.5),
        'b_conv_w': normal((N_EVEN, CONV_W, CONV_DIM), CONV_W ** -0.5),
        'b_conv_b': normal((N_EVEN, CONV_DIM), 0.01),
        'b_dt_bias': dt0 + jnp.log(-jnp.expm1(-dt0)),
        'b_a_log': jnp.log(a_init),
        'b_d': gain((N_EVEN, NH_B)),
        'b_norm_w': gain((N_EVEN, D_INNER)),
        'c_w_in': normal((N_ODD, D_MODEL, C_IN), D_MODEL ** -0.5),
        'c_w_out': normal((N_ODD, C_OUT, D_MODEL), C_OUT ** -0.5),
    }


def reference(x_prompt, x_sample, cache_a_k, cache_a_v, state_b_ssm, state_b_conv, cache_c1_k, cache_c1_v,
              cache_c2_k, cache_c2_v, cache_c3_k, cache_c3_v, rel_bias, norm_ff1, norm_mix, norm_ff2,
              norm_final, ff1_gate, ff1_up, ff1_down, ff2_gate, ff2_up, ff2_down, ab_w_in, ab_w_out, a_sinks,
              b_conv_w, b_conv_b, b_dt_bias, b_a_log, b_d, b_norm_w, c_w_in, c_w_out):
    c_cache = (cache_c1_k, cache_c1_v, cache_c2_k, cache_c2_v, cache_c3_k, cache_c3_v)

    def trunk(x, sample):
        ab_new, c_new = [], []
        for layer in range(DEPTH):
            x = x + 0.5 * swiglu(rmsnorm(x, norm_ff1[layer]), ff1_gate[layer], ff1_up[layer], ff1_down[layer])
            h = rmsnorm(x, norm_mix[layer])
            if layer % 2 == 0:
                e = layer // 2
                if sample:
                    kv, conv, ssm = (cache_a_k[e], cache_a_v[e]), state_b_conv[e], state_b_ssm[e]
                else:
                    kv, conv, ssm = None, None, None
                mix, st = ab_mixer(h, kv, conv, ssm, ab_w_in[e], ab_w_out[e], a_sinks[e], b_conv_w[e],
                                   b_conv_b[e], b_dt_bias[e], b_a_log[e], b_d[e], b_norm_w[e], rel_bias)
                ab_new.append(st)
            else:
                o = layer // 2
                past = tuple(c[o] for c in c_cache) if sample else None
                mix, st = c_mixer(h, past, c_w_in[o], c_w_out[o], rel_bias)
                c_new.append(st)
            x = x + mix
            x = x + 0.5 * swiglu(rmsnorm(x, norm_ff2[layer]), ff2_gate[layer], ff2_up[layer], ff2_down[layer])
        ab_st = [jnp.stack([s[i] for s in ab_new]) for i in range(4)]
        c_st = [jnp.stack([s[i] for s in c_new]) for i in range(2 * N_GROUPS_C)]
        return rmsnorm(x, norm_final), ab_st + c_st

    y_prompt, (p_a_k, p_a_v, p_b_ssm, p_b_conv, p_c1_k, p_c1_v, p_c2_k, p_c2_v, p_c3_k, p_c3_v) = trunk(x_prompt, False)
    y_sample, (s_a_k, s_a_v, s_b_ssm, s_b_conv, s_c1_k, s_c1_v, s_c2_k, s_c2_v, s_c3_k, s_c3_v) = trunk(x_sample, True)
    return (y_prompt, y_sample,
            p_a_k, p_a_v, p_b_ssm, p_b_conv, p_c1_k, p_c1_v, p_c2_k, p_c2_v, p_c3_k, p_c3_v,
            s_a_k, s_a_v, s_b_ssm, s_b_conv, s_c1_k, s_c1_v, s_c2_k, s_c2_v, s_c3_k, s_c3_v)
```

```python
import functools
import math

import jax
import jax.numpy as jnp
from jax import lax
from jax.experimental import pallas as pl
from jax.experimental.pallas import tpu as pltpu

F32 = jnp.float32
BF16 = jnp.bfloat16

D_MODEL = 1024
DEPTH = 4
NUM_BUCKETS = 32
MAX_DISTANCE = 2048
N_ATT_HEADS = 12
Q_BLOCK = 128
RMS_EPS = 1e-6
NEG_INF = -1e30
D_FF = 2816
HQ_A, HKV_A, DH_A, WIN_A = 12, 4, 64, 128
D_INNER = 2 * D_MODEL
P_B = 64
NH_B = D_INNER // P_B
N_B = 128
NG_B = 4
CONV_W = 4
CONV_DIM = D_INNER + 2 * NG_B * N_B
SSD_CHUNK = 128
C_GROUPS = ((128, 1), (512, 4), (2048, 16))
N_GROUPS_C = 3
HPG_C = 4
DH_C = 128
AB_Q = HQ_A * DH_A
AB_KV = HKV_A * DH_A
AB_SPLITS = (AB_Q, AB_Q + AB_KV, AB_Q + 2 * AB_KV, AB_Q + 2 * AB_KV + D_INNER,
             AB_Q + 2 * AB_KV + D_INNER + CONV_DIM)
AB_IN = AB_SPLITS[-1] + NH_B
AB_OUT = AB_Q + D_INNER
C_IN = 3 * N_ATT_HEADS * DH_C
C_OUT = HPG_C * DH_C

VMEM_LIMIT = 56 * 1024 * 1024


def _rms(x, w):
    ms = jnp.mean(x * x, axis=-1, keepdims=True)
    return x * lax.rsqrt(ms + RMS_EPS) * w


def _ffn_kernel(x_ref, nw_ref, wg_ref, wu_ref, wd_ref, o_ref, h_ref, acc_ref):
    j = pl.program_id(1)

    @pl.when(j == 0)
    def _():
        h_ref[...] = _rms(x_ref[...], nw_ref[...]).astype(BF16)
        acc_ref[...] = jnp.zeros_like(acc_ref)

    h = h_ref[...]
    g = jnp.dot(h, wg_ref[...], preferred_element_type=F32)
    u = jnp.dot(h, wu_ref[...], preferred_element_type=F32)
    a = (g * jax.nn.sigmoid(g) * u).astype(BF16)
    acc_ref[...] += jnp.dot(a, wd_ref[...], preferred_element_type=F32)

    @pl.when(j == pl.num_programs(1) - 1)
    def _():
        o_ref[...] = x_ref[...] + 0.5 * acc_ref[...]


def _ffn(x, nw, wg, wu, wd, tm=1024, tf=256):
    t, d = x.shape
    f = wg.shape[1]
    return pl.pallas_call(
        _ffn_kernel,
        out_shape=jax.ShapeDtypeStruct((t, d), F32),
        grid=(t // tm, f // tf),
        in_specs=[
            pl.BlockSpec((tm, d), lambda i, j: (i, 0)),
            pl.BlockSpec((1, d), lambda i, j: (0, 0)),
            pl.BlockSpec((d, tf), lambda i, j: (0, j)),
            pl.BlockSpec((d, tf), lambda i, j: (0, j)),
            pl.BlockSpec((tf, d), lambda i, j: (j, 0)),
        ],
        out_specs=pl.BlockSpec((tm, d), lambda i, j: (i, 0)),
        scratch_shapes=[pltpu.VMEM((tm, d), BF16), pltpu.VMEM((tm, d), F32)],
        compiler_params=pltpu.CompilerParams(
            dimension_semantics=("parallel", "arbitrary"), vmem_limit_bytes=VMEM_LIMIT),
        name="ffn",
    )(x, nw.reshape(1, d), wg, wu, wd)


def _rms_matmul_kernel(x_ref, nw_ref, w_ref, o_ref, h_ref):
    @pl.when(pl.program_id(1) == 0)
    def _():
        h_ref[...] = _rms(x_ref[...], nw_ref[...]).astype(BF16)

    o_ref[...] = jnp.dot(h_ref[...], w_ref[...], preferred_element_type=F32).astype(o_ref.dtype)


def _rms_matmul(x, nw, w, tn, tm=1024, out_dtype=F32):
    t, d = x.shape
    n = w.shape[1]
    return pl.pallas_call(
        _rms_matmul_kernel,
        out_shape=jax.ShapeDtypeStruct((t, n), out_dtype),
        grid=(t // tm, n // tn),
        in_specs=[
            pl.BlockSpec((tm, d), lambda i, j: (i, 0)),
            pl.BlockSpec((1, d), lambda i, j: (0, 0)),
            pl.BlockSpec((d, tn), lambda i, j: (0, j)),
        ],
        out_specs=pl.BlockSpec((tm, tn), lambda i, j: (i, j)),
        scratch_shapes=[pltpu.VMEM((tm, d), BF16)],
        compiler_params=pltpu.CompilerParams(
            dimension_semantics=("parallel", "arbitrary"), vmem_limit_bytes=VMEM_LIMIT),
        name="rms_matmul",
    )(x, nw.reshape(1, d), w)


def _matmul_res_kernel(a_ref, w_ref, x_ref, o_ref):
    o_ref[...] = x_ref[...] + jnp.dot(a_ref[...].astype(BF16), w_ref[...], preferred_element_type=F32)


def _matmul_res(a, w, x, tm=512):
    t, k = a.shape
    d = w.shape[1]
    return pl.pallas_call(
        _matmul_res_kernel,
        out_shape=jax.ShapeDtypeStruct((t, d), F32),
        grid=(t // tm,),
        in_specs=[
            pl.BlockSpec((tm, k), lambda i: (i, 0)),
            pl.BlockSpec((k, d), lambda i: (0, 0)),
            pl.BlockSpec((tm, d), lambda i: (i, 0)),
        ],
        out_specs=pl.BlockSpec((tm, d), lambda i: (i, 0)),
        compiler_params=pltpu.CompilerParams(
            dimension_semantics=("parallel",), vmem_limit_bytes=VMEM_LIMIT),
        name="matmul_res",
    )(a, w, x)


def t5_bucket(dist):
    n = jnp.maximum(dist, 0)
    max_exact = NUM_BUCKETS // 2
    nf = jnp.maximum(n, 1).astype(jnp.float32)
    large = max_exact + (jnp.log(nf / max_exact) / math.log(MAX_DISTANCE / max_exact)
                         * (NUM_BUCKETS - max_exact)).astype(jnp.int32)
    large = jnp.minimum(large, NUM_BUCKETS - 1)
    return jnp.where(n < max_exact, n, large)


def softmax_with_sink(logits, sink):
    m = jnp.max(logits, axis=-1, keepdims=True)
    if sink is not None:
        m = jnp.maximum(m, sink)
    p = jnp.exp(logits - m)
    denom = jnp.sum(p, axis=-1, keepdims=True)
    if sink is not None:
        denom = denom + jnp.exp(sink - m)
    return p / denom, (m + jnp.log(denom))[..., 0]


def banded_window_attn(q, k, v, bias_heads, dil, n_win, sinks):
    b, L, hq, dh = q.shape
    hkv = k.shape[2]
    g = hq // hkv
    nb = -(-L // Q_BLOCK)
    pad = nb * Q_BLOCK - L
    padw = ((0, 0), (0, pad), (0, 0), (0, 0))
    q, k, v = (jnp.pad(t, padw) for t in (q, k, v))
    qb = q.reshape(b, nb, Q_BLOCK, hkv, g, dh)
    kb = k.reshape(b, nb, Q_BLOCK, hkv, dh)
    vb = v.reshape(b, nb, Q_BLOCK, hkv, dh)
    prev = ((0, 0), (1, 0), (0, 0), (0, 0), (0, 0))
    kk = jnp.concatenate([jnp.pad(kb[:, :-1], prev), kb], axis=2)
    vv = jnp.concatenate([jnp.pad(vb[:, :-1], prev), vb], axis=2)
    logits = jnp.einsum('bnqkgd,bnskd->bnkgqs', qb, kk).astype(jnp.float32) * (dh ** -0.5)
    qi = jnp.arange(Q_BLOCK)[:, None]
    si = jnp.arange(2 * Q_BLOCK)[None, :]
    dist = qi + Q_BLOCK - si
    key_idx = jnp.arange(nb)[:, None, None] * Q_BLOCK - Q_BLOCK + si
    valid = (dist >= 0) & (dist <= n_win) & (key_idx >= 0)
    bias = bias_heads[t5_bucket(dist * dil)].astype(jnp.float32)
    bias = jnp.transpose(bias, (2, 0, 1)).reshape(hkv, g, Q_BLOCK, 2 * Q_BLOCK)
    logits = jnp.where(valid[None, :, None, None], logits + bias, NEG_INF)
    sink = None if sinks is None else sinks.astype(jnp.float32).reshape(hkv, g, 1, 1)
    probs, lse = softmax_with_sink(logits, sink)
    out = jnp.einsum('bnkgqs,bnskd->bnqkgd', probs.astype(v.dtype), vv)
    out = out.reshape(b, nb * Q_BLOCK, hq, dh)[:, :L]
    lse = jnp.transpose(lse, (0, 1, 4, 2, 3)).reshape(b, nb * Q_BLOCK, hq)[:, :L]
    return out, lse


def dilated_prompt_attn(q, k, v, bias_heads, dil, n_win):
    b, L, h, dh = q.shape

    def fold(t):
        return jnp.transpose(t.reshape(b, L // dil, dil, h, dh), (0, 2, 1, 3, 4)).reshape(b * dil, L // dil, h, dh)

    out, lse = banded_window_attn(fold(q), fold(k), fold(v), bias_heads, dil, n_win, None)
    out = jnp.transpose(out.reshape(b, dil, L // dil, h, dh), (0, 2, 1, 3, 4)).reshape(b, L, h, dh)
    lse = jnp.transpose(lse.reshape(b, dil, L // dil, h), (0, 2, 1, 3)).reshape(b, L, h)
    return out, lse


def gathered_window_attn(q, k_all, v_all, bias_heads, dil, n_win, sinks):
    b, t, hq, dh = q.shape
    l_tot, hkv = k_all.shape[1], k_all.shape[2]
    g = hq // hkv
    dist = jnp.arange(n_win + 1) * dil
    idx = (l_tot - t) + jnp.arange(t)[:, None] - dist[None, :]
    valid = idx >= 0
    idx = jnp.clip(idx, 0, l_tot - 1)
    kg = k_all[:, idx]
    vg = v_all[:, idx]
    qg = q.reshape(b, t, hkv, g, dh)
    logits = jnp.einsum('btkgd,btjkd->bkgtj', qg, kg).astype(jnp.float32) * (dh ** -0.5)
    bias = bias_heads[t5_bucket(dist)].astype(jnp.float32).T.reshape(hkv, g, 1, n_win + 1)
    logits = jnp.where(valid, logits + bias, NEG_INF)
    sink = None if sinks is None else sinks.astype(jnp.float32).reshape(hkv, g, 1, 1)
    probs, lse = softmax_with_sink(logits, sink)
    out = jnp.einsum('bkgtj,btjkd->btkgd', probs.astype(v_all.dtype), vg).reshape(b, t, hq, dh)
    lse = jnp.transpose(lse, (0, 3, 1, 2)).reshape(b, t, hq)
    return out, lse


def causal_dwconv(u, state, w, bias):
    up = jnp.concatenate([state.astype(u.dtype), u], axis=1)
    out = lax.conv_general_dilated(up, w[:, None, :].astype(u.dtype), window_strides=(1,), padding='VALID',
                                   dimension_numbers=('NWC', 'WIO', 'NWC'), feature_group_count=u.shape[-1])
    return out + bias.astype(u.dtype), up[:, -(CONV_W - 1):]


def ssd_scan(x, dt, a, bm, cm, h0):
    f32 = jnp.float32
    b, L, nh, p = x.shape
    ng, n = bm.shape[2], bm.shape[3]
    hg = nh // ng
    cl = min(SSD_CHUNK, L)
    nc = -(-L // cl)
    pad = nc * cl - L
    x = jnp.pad(x.astype(f32), ((0, 0), (0, pad), (0, 0), (0, 0))).reshape(b, nc, cl, ng, hg, p)
    dt = jnp.pad(dt.astype(f32), ((0, 0), (0, pad), (0, 0))).reshape(b, nc, cl, ng, hg)
    bm = jnp.pad(bm.astype(f32), ((0, 0), (0, pad), (0, 0), (0, 0))).reshape(b, nc, cl, ng, n)
    cm = jnp.pad(cm.astype(f32), ((0, 0), (0, pad), (0, 0), (0, 0))).reshape(b, nc, cl, ng, n)
    a_cum = jnp.cumsum(dt * a.reshape(ng, hg), axis=2)
    seg = a_cum[:, :, :, None] - a_cum[:, :, None, :]
    causal = jnp.tril(jnp.ones((cl, cl), dtype=bool))[:, :, None, None]
    decay = jnp.exp(jnp.where(causal, seg, NEG_INF))
    cb = jnp.einsum('bclgn,bcsgn->bclsg', cm, bm)
    y_diag = jnp.einsum('bclsgh,bcsghp->bclghp', cb[..., None] * decay * dt[:, :, None], x)
    decay_end = jnp.exp(a_cum[:, :, -1:] - a_cum) * dt
    chunk_states = jnp.einsum('bclgh,bclghp,bclgn->bcghpn', decay_end, x, bm)
    chunk_decay = jnp.exp(a_cum[:, :, -1])

    def step(h, inp):
        s, dcy = inp
        return dcy[..., None, None] * h + s, h

    h_last, h_starts = lax.scan(step, h0.astype(f32).reshape(b, ng, hg, p, n),
                                (jnp.moveaxis(chunk_states, 1, 0), jnp.moveaxis(chunk_decay, 1, 0)))
    h_starts = jnp.moveaxis(h_starts, 0, 1)
    y_off = jnp.einsum('bclgn,bcghpn->bclghp', cm, h_starts) * jnp.exp(a_cum)[..., None]
    y = (y_diag + y_off).reshape(b, nc * cl, nh, p)[:, :L]
    return y, h_last.reshape(b, nh, p, n)


def gated_group_rmsnorm(y, z, w):
    b, L, d = y.shape
    g = (y.astype(jnp.float32) * jax.nn.silu(z.astype(jnp.float32))).reshape(b, L, NG_B, d // NG_B)
    g = g * lax.rsqrt(jnp.mean(g * g, axis=-1, keepdims=True) + RMS_EPS)
    return (g.reshape(b, L, d) * w.astype(jnp.float32)).astype(z.dtype)


def ab_mixer_core(proj, kv_past, conv_state, ssm_state, sinks, conv_w, conv_b, dt_bias, a_log,
                  d_skip, norm_w, rel_bias):
    b, L, _ = proj.shape
    q, k, v, z, xbc, dt = jnp.split(proj, AB_SPLITS, axis=-1)
    q = q.reshape(b, L, HQ_A, DH_A)
    k = k.reshape(b, L, HKV_A, DH_A)
    v = v.reshape(b, L, HKV_A, DH_A)
    bias_heads = rel_bias[:, :HQ_A]
    if kv_past is None:
        att, _ = banded_window_attn(q, k, v, bias_heads, 1, WIN_A, sinks)
        k_all, v_all, keep = k, v, min(WIN_A, L)
        conv_state = jnp.zeros((b, CONV_W - 1, CONV_DIM), proj.dtype)
        ssm_state = jnp.zeros((b, NH_B, P_B, N_B), jnp.float32)
    else:
        k_all = jnp.concatenate([kv_past[0].astype(k.dtype), k], axis=1)
        v_all = jnp.concatenate([kv_past[1].astype(v.dtype), v], axis=1)
        keep = kv_past[0].shape[1]
        att, _ = gathered_window_attn(q, k_all, v_all, bias_heads, 1, WIN_A, sinks)
    xbc, new_conv = causal_dwconv(xbc, conv_state, conv_w, conv_b)
    xbc = jax.nn.silu(xbc)
    xs, bm, cm = jnp.split(xbc, [D_INNER, D_INNER + NG_B * N_B], axis=-1)
    xs = xs.reshape(b, L, NH_B, P_B)
    dt = jax.nn.softplus(dt.astype(jnp.float32) + dt_bias.astype(jnp.float32))
    a = -jnp.exp(a_log.astype(jnp.float32))
    y, new_ssm = ssd_scan(xs, dt, a, bm.reshape(b, L, NG_B, N_B), cm.reshape(b, L, NG_B, N_B), ssm_state)
    y = y + d_skip.astype(jnp.float32)[:, None] * xs.astype(jnp.float32)
    y = gated_group_rmsnorm(y.reshape(b, L, D_INNER), z, norm_w)
    pre = jnp.concatenate([att.reshape(b, L, AB_Q), y.astype(att.dtype)], axis=-1)
    return pre, (k_all[:, -keep:], v_all[:, -keep:], new_ssm.astype(ssm_state.dtype), new_conv)


def c_mixer_core(proj, kv_past, rel_bias):
    b, L, _ = proj.shape
    q, k, v = (t.reshape(b, L, N_GROUPS_C, HPG_C, DH_C) for t in jnp.split(proj, 3, axis=-1))
    outs, lses, new_rows = [], [], []
    for gi, (win, dil) in enumerate(C_GROUPS):
        n_win = win // dil
        qg, kg, vg = q[:, :, gi], k[:, :, gi], v[:, :, gi]
        bias_heads = rel_bias[:, gi * HPG_C:(gi + 1) * HPG_C]
        if kv_past is None:
            o, lse = dilated_prompt_attn(qg, kg, vg, bias_heads, dil, n_win)
            k_all, v_all, keep = kg, vg, min(win, L)
        else:
            k_past, v_past = kv_past[2 * gi], kv_past[2 * gi + 1]
            k_all = jnp.concatenate([k_past.astype(kg.dtype), kg], axis=1)
            v_all = jnp.concatenate([v_past.astype(vg.dtype), vg], axis=1)
            keep = k_past.shape[1]
            o, lse = gathered_window_attn(qg, k_all, v_all, bias_heads, dil, n_win, None)
        outs.append(o)
        lses.append(lse)
        new_rows += [k_all[:, -keep:], v_all[:, -keep:]]
    wts = jax.nn.softmax(jnp.stack(lses, axis=-1), axis=-1)
    o = jnp.einsum('blhgd,blhg->blhd', jnp.stack(outs, axis=3), wts.astype(outs[0].dtype))
    return o.reshape(b, L, C_OUT), tuple(new_rows)


def kernel(x_prompt, x_sample, cache_a_k, cache_a_v, state_b_ssm, state_b_conv, cache_c1_k, cache_c1_v,
           cache_c2_k, cache_c2_v, cache_c3_k, cache_c3_v, rel_bias, norm_ff1, norm_mix, norm_ff2,
           norm_final, ff1_gate, ff1_up, ff1_down, ff2_gate, ff2_up, ff2_down, ab_w_in, ab_w_out, a_sinks,
           b_conv_w, b_conv_b, b_dt_bias, b_a_log, b_d, b_norm_w, c_w_in, c_w_out):
    bp, lp, d = x_prompt.shape
    bs, ls, _ = x_sample.shape
    tp, ts = bp * lp, bs * ls
    c_cache = (cache_c1_k, cache_c1_v, cache_c2_k, cache_c2_v, cache_c3_k, cache_c3_v)
    x = jnp.concatenate([x_prompt.reshape(tp, d), x_sample.reshape(ts, d)], axis=0)

    ab_pad = (-AB_IN) % 128
    ab_new_p, ab_new_s, c_new_p, c_new_s = [], [], [], []
    for layer in range(DEPTH):
        x = _ffn(x, norm_ff1[layer], ff1_gate[layer].astype(BF16), ff1_up[layer].astype(BF16),
                 ff1_down[layer].astype(BF16))
        if layer % 2 == 0:
            e = layer // 2
            w_in = jnp.pad(ab_w_in[e], ((0, 0), (0, ab_pad))).astype(BF16)
            proj = _rms_matmul(x, norm_mix[layer], w_in, tn=384)[:, :AB_IN]
            args = (a_sinks[e], b_conv_w[e], b_conv_b[e], b_dt_bias[e], b_a_log[e], b_d[e], b_norm_w[e], rel_bias)
            pre_p, st_p = ab_mixer_core(proj[:tp].reshape(bp, lp, AB_IN), None, None, None, *args)
            pre_s, st_s = ab_mixer_core(proj[tp:].reshape(bs, ls, AB_IN), (cache_a_k[e], cache_a_v[e]),
                                        state_b_conv[e], state_b_ssm[e], *args)
            ab_new_p.append(st_p)
            ab_new_s.append(st_s)
            pre = jnp.concatenate([pre_p.reshape(tp, AB_OUT), pre_s.reshape(ts, AB_OUT)], axis=0)
            x = _matmul_res(pre, ab_w_out[e].astype(BF16), x)
        else:
            o = layer // 2
            proj = _rms_matmul(x, norm_mix[layer], c_w_in[o].astype(BF16), tn=512)
            pre_p, st_p = c_mixer_core(proj[:tp].reshape(bp, lp, C_IN), None, rel_bias)
            pre_s, st_s = c_mixer_core(proj[tp:].reshape(bs, ls, C_IN), tuple(c[o] for c in c_cache), rel_bias)
            c_new_p.append(st_p)
            c_new_s.append(st_s)
            pre = jnp.concatenate([pre_p.reshape(tp, C_OUT), pre_s.reshape(ts, C_OUT)], axis=0)
            x = _matmul_res(pre, c_w_out[o].astype(BF16), x)
        x = _ffn(x, norm_ff2[layer], ff2_gate[layer].astype(BF16), ff2_up[layer].astype(BF16),
                 ff2_down[layer].astype(BF16))

    y = _rms(x, norm_final.reshape(1, d))
    y_prompt = y[:tp].reshape(bp, lp, d)
    y_sample = y[tp:].reshape(bs, ls, d)

    def stack(states, n):
        return [jnp.stack([s[i] for s in states]) for i in range(n)]

    return (y_prompt, y_sample, *stack(ab_new_p, 4), *stack(c_new_p, 6), *stack(ab_new_s, 4), *stack(c_new_s, 6))
```

```python
import functools
import math

import jax
import jax.numpy as jnp
from jax import lax
from jax.experimental import pallas as pl
from jax.experimental.pallas import tpu as pltpu

F32 = jnp.float32
BF16 = jnp.bfloat16

D_MODEL = 1024
DEPTH = 4
NUM_BUCKETS = 32
MAX_DISTANCE = 2048
Q_BLOCK = 128
RMS_EPS = 1e-6
NEG_INF = -1e30
HQ_A, HKV_A, DH_A, WIN_A = 12, 4, 64, 128
D_INNER = 2 * D_MODEL
P_B = 64
NH_B = D_INNER // P_B
N_B = 128
NG_B = 4
HPG_B = NH_B // NG_B
GW_B = HPG_B * P_B
CONV_W = 4
CONV_DIM = D_INNER + 2 * NG_B * N_B
SSD_CHUNK = 128
C_GROUPS = ((128, 1), (512, 4), (2048, 16))
HPG_C = 4
DH_C = 128
C_OUT = HPG_C * DH_C
AB_Q = HQ_A * DH_A
AB_KV = HKV_A * DH_A
C_IN = 3 * 3 * C_OUT

ABP_Q, ABP_K, ABP_V, ABP_DT = 0, AB_Q, AB_Q + AB_KV, AB_Q + 2 * AB_KV
ABP_Z = ABP_DT + 256
ABP_XS = ABP_Z + D_INNER
ABP_B = ABP_XS + D_INNER
ABP_C = ABP_B + NG_B * N_B
ABP_N = ABP_C + NG_B * N_B

SUBLANES = 8
LANES = 128
VMEM_LIMIT = 56 * 1024 * 1024


def _cparams(n_axes):
    return pltpu.CompilerParams(dimension_semantics=("arbitrary",) * n_axes, vmem_limit_bytes=VMEM_LIMIT)


def _pick(n, options):
    for o in options:
        if n % o == 0:
            return o
    raise ValueError(f"no tile in {options} divides {n}")


def _rms(x, w):
    ms = jnp.mean(x * x, axis=-1, keepdims=True)
    return x * lax.rsqrt(ms + RMS_EPS) * w


def _silu(x):
    return x * jax.nn.sigmoid(x)


def _ffn_kernel(x_ref, nw_ref, wg_ref, wu_ref, wd_ref, fw_ref, o_ref, h_ref, acc_ref, *, final_norm):
    j = pl.program_id(1)

    @pl.when(j == 0)
    def _():
        h_ref[...] = _rms(x_ref[...], nw_ref[...]).astype(BF16)
        acc_ref[...] = jnp.zeros_like(acc_ref)

    h = h_ref[...]
    g = jnp.dot(h, wg_ref[...], preferred_element_type=F32)
    u = jnp.dot(h, wu_ref[...], preferred_element_type=F32)
    a = (_silu(g) * u).astype(BF16)
    acc_ref[...] += jnp.dot(a, wd_ref[...], preferred_element_type=F32)

    @pl.when(j == pl.num_programs(1) - 1)
    def _():
        y = x_ref[...] + 0.5 * acc_ref[...]
        if final_norm:
            y = _rms(y, fw_ref[...])
        o_ref[...] = y


def _ffn(x, nw, wg, wu, wd, final_w=None, tf=256):
    t, d = x.shape
    f = wg.shape[1]
    tm = _pick(t, (1024, 512, 256, 128))
    fw = jnp.ones((d,), F32) if final_w is None else final_w
    return pl.pallas_call(
        functools.partial(_ffn_kernel, final_norm=final_w is not None),
        out_shape=jax.ShapeDtypeStruct((t, d), F32),
        grid=(t // tm, f // tf),
        in_specs=[
            pl.BlockSpec((tm, d), lambda i, j: (i, 0)),
            pl.BlockSpec((1, d), lambda i, j: (0, 0)),
            pl.BlockSpec((d, tf), lambda i, j: (0, j)),
            pl.BlockSpec((d, tf), lambda i, j: (0, j)),
            pl.BlockSpec((tf, d), lambda i, j: (j, 0)),
            pl.BlockSpec((1, d), lambda i, j: (0, 0)),
        ],
        out_specs=pl.BlockSpec((tm, d), lambda i, j: (i, 0)),
        scratch_shapes=[pltpu.VMEM((tm, d), BF16), pltpu.VMEM((tm, d), F32)],
        compiler_params=_cparams(2),
        name="ffn",
    )(x, nw.reshape(1, d), wg, wu, wd, fw.reshape(1, d))


def _rms_matmul_kernel(x_ref, nw_ref, w_ref, o_ref, h_ref):
    @pl.when(pl.program_id(1) == 0)
    def _():
        h_ref[...] = _rms(x_ref[...], nw_ref[...]).astype(BF16)

    o_ref[...] = jnp.dot(h_ref[...], w_ref[...], preferred_element_type=F32)


def _rms_matmul(x, nw, w, tn=512):
    t, d = x.shape
    n = w.shape[1]
    tm = _pick(t, (1024, 512, 256, 128))
    return pl.pallas_call(
        _rms_matmul_kernel,
        out_shape=jax.ShapeDtypeStruct((t, n), F32),
        grid=(t // tm, n // tn),
        in_specs=[
            pl.BlockSpec((tm, d), lambda i, j: (i, 0)),
            pl.BlockSpec((1, d), lambda i, j: (0, 0)),
            pl.BlockSpec((d, tn), lambda i, j: (0, j)),
        ],
        out_specs=pl.BlockSpec((tm, tn), lambda i, j: (i, j)),
        scratch_shapes=[pltpu.VMEM((tm, d), BF16)],
        compiler_params=_cparams(2),
        name="rms_matmul",
    )(x, nw.reshape(1, d), w)


def _t5_bucket(dist):
    n = jnp.maximum(dist, 0)
    max_exact = NUM_BUCKETS // 2
    nf = jnp.maximum(n, 1).astype(F32)
    large = max_exact + (jnp.log(nf / max_exact) / math.log(MAX_DISTANCE / max_exact)
                         * (NUM_BUCKETS - max_exact)).astype(jnp.int32)
    large = jnp.minimum(large, NUM_BUCKETS - 1)
    return jnp.where(n < max_exact, n, large)


def _band_bias(bias_heads, dil, n_win):
    qi = jnp.arange(Q_BLOCK)[:, None]
    si = jnp.arange(2 * Q_BLOCK)[None, :]
    dist = qi + Q_BLOCK - si
    valid = (dist >= 0) & (dist <= n_win)
    bias = jnp.transpose(bias_heads[_t5_bucket(dist * dil)].astype(F32), (2, 0, 1))
    return jnp.where(valid[None], bias, NEG_INF)


def _step_bias(bias_heads, lbuf, ls, dil, n_win):
    nh = bias_heads.shape[1]
    i = jnp.arange(ls)[:, None]

    def table(delta, ok):
        ok = ok & (delta >= 0) & (delta % dil == 0) & (delta // dil <= n_win)
        b = jnp.transpose(bias_heads[_t5_bucket(delta)].astype(F32), (2, 0, 1))
        return jnp.where(ok[None], b, NEG_INF).reshape(nh * ls, delta.shape[1])

    c = jnp.arange(lbuf)[None, :]
    cn = jnp.arange(LANES)[None, :]
    return table(lbuf + i - c, c >= 0), table(i - cn, cn < ls)


def _band_attn_kernel(*refs, hq, hkv, dh, has_sink, emit_lse):
    it = iter(refs)
    q_ref, kp_ref, kc_ref, vp_ref, vc_ref, bias_ref = (next(it) for _ in range(6))
    sink_ref = next(it) if has_sink else None
    o_ref = next(it)
    lse_ref = next(it) if emit_lse else None
    g = hq // hkv
    scale = dh ** -0.5
    col = lax.broadcasted_iota(jnp.int32, (Q_BLOCK, 2 * Q_BLOCK), 1)
    no_prev = col < jnp.where(pl.program_id(2) == 0, Q_BLOCK, 0)
    outs, lses = [], []
    for kv in range(hkv):
        sl = slice(kv * dh, (kv + 1) * dh)
        k = jnp.concatenate([kp_ref[:, sl], kc_ref[:, sl]], axis=0).astype(BF16)
        v = jnp.concatenate([vp_ref[:, sl], vc_ref[:, sl]], axis=0).astype(BF16)
        for gi in range(g):
            h = kv * g + gi
            qh = (q_ref[:, h * dh:(h + 1) * dh] * scale).astype(BF16)
            s = lax.dot_general(qh, k, (((1,), (1,)), ((), ())), preferred_element_type=F32)
            s = jnp.where(no_prev, NEG_INF, s + bias_ref[h])
            m = jnp.max(s, axis=-1, keepdims=True)
            if has_sink:
                m = jnp.maximum(m, sink_ref[h])
            p = jnp.exp(s - m)
            l = jnp.sum(p, axis=-1, keepdims=True)
            if has_sink:
                l = l + jnp.exp(sink_ref[h] - m)
            o = jnp.dot(p.astype(BF16), v, preferred_element_type=F32)
            outs.append(o * (1.0 / l))
            lses.append(m + jnp.log(l))
    o_ref[...] = jnp.concatenate(outs, axis=1).astype(o_ref.dtype)
    if emit_lse:
        lse_ref[...] = _pack_heads(lses)


def _pack_heads(cols):
    n = len(cols)
    rows = cols[0].shape[0]
    lane = lax.broadcasted_iota(jnp.int32, (rows, LANES), 1)
    out = jnp.broadcast_to(cols[-1], (rows, LANES))
    for h in range(n - 2, -1, -1):
        out = jnp.where(lane < (h + 1) * (LANES // n), cols[h], out)
    return out


def _unpack_heads(packed, n):
    rows = packed.shape[0]
    w = LANES // n
    return jnp.concatenate([jnp.broadcast_to(packed[:, h * w:h * w + 1], (rows, LANES)) for h in range(n)], axis=1)


def _band_attn(p, bias, sinks, *, nseq, seqlen, dil, hq, hkv, dh, q_start, k_start, v_start, emit_lse, out_dtype):
    t, ncols = p.shape
    wq, wk = hq * dh, hkv * dh
    assert seqlen % (dil * Q_BLOCK) == 0 and t % dil == 0
    assert q_start % wq == 0 and k_start % wk == 0 and v_start % wk == 0
    assert dil == 1 or (ncols % wq == 0 and ncols % wk == 0)
    nbd = seqlen // dil // Q_BLOCK
    pv = p.reshape(t // dil, dil * ncols)
    tq = nseq * seqlen // dil

    def cur(start, w):
        return lambda b, r, n: (b * nbd + n, r * (ncols // w) + start // w)

    def prev(start, w):
        return lambda b, r, n: (b * nbd + jnp.maximum(n - 1, 0), r * (ncols // w) + start // w)

    in_specs = [
        pl.BlockSpec((Q_BLOCK, wq), cur(q_start, wq)),
        pl.BlockSpec((Q_BLOCK, wk), prev(k_start, wk)),
        pl.BlockSpec((Q_BLOCK, wk), cur(k_start, wk)),
        pl.BlockSpec((Q_BLOCK, wk), prev(v_start, wk)),
        pl.BlockSpec((Q_BLOCK, wk), cur(v_start, wk)),
        pl.BlockSpec((hq, Q_BLOCK, 2 * Q_BLOCK), lambda b, r, n: (0, 0, 0)),
    ]
    args = [pv, pv, pv, pv, pv, bias]
    if sinks is not None:
        in_specs.append(pl.BlockSpec(memory_space=pltpu.SMEM))
        args.append(sinks.astype(F32))
    out_shape = [jax.ShapeDtypeStruct((tq, dil * wq), out_dtype)]
    out_specs = [pl.BlockSpec((Q_BLOCK, wq), lambda b, r, n: (b * nbd + n, r))]
    if emit_lse:
        out_shape.append(jax.ShapeDtypeStruct((tq, dil * LANES), F32))
        out_specs.append(pl.BlockSpec((Q_BLOCK, LANES), lambda b, r, n: (b * nbd + n, r)))
    res = pl.pallas_call(
        functools.partial(_band_attn_kernel, hq=hq, hkv=hkv, dh=dh, has_sink=sinks is not None, emit_lse=emit_lse),
        out_shape=out_shape,
        grid=(nseq, dil, nbd),
        in_specs=in_specs,
        out_specs=out_specs,
        compiler_params=_cparams(3),
        name=f"band_attn_d{dil}",
    )(*args)
    o = res[0].reshape(nseq * seqlen, wq)
    if emit_lse:
        return o, res[1].reshape(nseq * seqlen, LANES)
    return o


def _step_attn_kernel(*refs, ls, hq, hkv, dh, has_sink, emit_lse):
    it = iter(refs)
    q_ref, kn_ref, vn_ref, ck_ref, cv_ref, bc_ref, bn_ref = (next(it) for _ in range(7))
    sink_ref = next(it) if has_sink else None
    o_ref = next(it)
    lse_ref = next(it) if emit_lse else None
    ok_ref, ov_ref = next(it), next(it)
    lbuf = ck_ref.shape[1]
    w = hkv * dh
    g = hq // hkv
    kn, vn = kn_ref[...], vn_ref[...]

    ok_ref[0, :lbuf - ls, :] = ck_ref[0, ls:, :]
    ok_ref[0, lbuf - ls:, :] = kn
    ov_ref[0, :lbuf - ls, :] = cv_ref[0, ls:, :]
    ov_ref[0, lbuf - ls:, :] = vn

    q = q_ref[...] * (dh ** -0.5)
    if g == 1:
        lane = lax.broadcasted_iota(jnp.int32, (ls, w), 1)
        head_of_lane = [(lane >= h * dh) & (lane < (h + 1) * dh) for h in range(hq)]
        qbd = jnp.concatenate([jnp.where(head_of_lane[h], q, 0.0) for h in range(hq)], axis=0)
    else:
        blocks = []
        for h in range(hq):
            kv = h // g
            parts = []
            if kv > 0:
                parts.append(jnp.zeros((ls, kv * dh), F32))
            parts.append(q[:, h * dh:(h + 1) * dh])
            if kv < hkv - 1:
                parts.append(jnp.zeros((ls, (hkv - 1 - kv) * dh), F32))
            blocks.append(jnp.concatenate(parts, axis=1))
        qbd = jnp.concatenate(blocks, axis=0)
    qbd = qbd.astype(BF16)

    pad = jnp.zeros((LANES - ls, w), F32)
    knp = jnp.concatenate([kn, pad], axis=0).astype(BF16)
    vnp = jnp.concatenate([vn, pad], axis=0).astype(BF16)
    nt = (((1,), (1,)), ((), ()))
    sc = lax.dot_general(qbd, ck_ref[0].astype(BF16), nt, preferred_element_type=F32) + bc_ref[...]
    sn = lax.dot_general(qbd, knp, nt, preferred_element_type=F32) + bn_ref[...]
    m = jnp.maximum(jnp.max(sc, axis=-1, keepdims=True), jnp.max(sn, axis=-1, keepdims=True))
    if has_sink:
        m = jnp.maximum(m, sink_ref[...])
    pc = jnp.exp(sc - m)
    pn = jnp.exp(sn - m)
    l = jnp.sum(pc, axis=-1, keepdims=True) + jnp.sum(pn, axis=-1, keepdims=True)
    if has_sink:
        l = l + jnp.exp(sink_ref[...] - m)
    of = (jnp.dot(pc.astype(BF16), cv_ref[0].astype(BF16), preferred_element_type=F32)
          + jnp.dot(pn.astype(BF16), vnp, preferred_element_type=F32)) * (1.0 / l)

    if g == 1:
        o = jnp.where(head_of_lane[0], of[:ls], 0.0)
        for h in range(1, hq):
            o = o + jnp.where(head_of_lane[h], of[h * ls:(h + 1) * ls], 0.0)
    else:
        o = jnp.concatenate([of[h * ls:(h + 1) * ls, (h // g) * dh:(h // g + 1) * dh] for h in range(hq)], axis=1)
    o_ref[...] = o
    if emit_lse:
        lse = m + jnp.log(l)
        lse_ref[...] = _pack_heads([lse[h * ls:(h + 1) * ls] for h in range(hq)])


def _step_attn(p, cache_k, cache_v, bias_c, bias_n, sink_rows, *, row0, ls, hq, hkv, dh, q_start, k_start, v_start,
               emit_lse):
    bs, lbuf, w = cache_k.shape
    wq = hq * dh
    assert ls == SUBLANES and row0 % ls == 0 and w == hkv * dh and lbuf % ls == 0
    assert q_start % wq == 0 and k_start % w == 0 and v_start % w == 0
    r0 = row0 // ls
    in_specs = [
        pl.BlockSpec((ls, wq), lambda b: (r0 + b, q_start // wq)),
        pl.BlockSpec((ls, w), lambda b: (r0 + b, k_start // w)),
        pl.BlockSpec((ls, w), lambda b: (r0 + b, v_start // w)),
        pl.BlockSpec((1, lbuf, w), lambda b: (b, 0, 0)),
        pl.BlockSpec((1, lbuf, w), lambda b: (b, 0, 0)),
        pl.BlockSpec((hq * ls, lbuf), lambda b: (0, 0)),
        pl.BlockSpec((hq * ls, LANES), lambda b: (0, 0)),
    ]
    args = [p, p, p, cache_k, cache_v, bias_c, bias_n]
    if sink_rows is not None:
        in_specs.append(pl.BlockSpec((hq * ls, 1), lambda b: (0, 0)))
        args.append(sink_rows)
    out_shape = [jax.ShapeDtypeStruct((bs * ls, wq), F32)]
    out_specs = [pl.BlockSpec((ls, wq), lambda b: (b, 0))]
    if emit_lse:
        out_shape.append(jax.ShapeDtypeStruct((bs * ls, LANES), F32))
        out_specs.append(pl.BlockSpec((ls, LANES), lambda b: (b, 0)))
    out_shape += [jax.ShapeDtypeStruct((bs, lbuf, w), F32)] * 2
    out_specs += [pl.BlockSpec((1, lbuf, w), lambda b: (b, 0, 0))] * 2
    return pl.pallas_call(
        functools.partial(_step_attn_kernel, ls=ls, hq=hq, hkv=hkv, dh=dh, has_sink=sink_rows is not None,
                          emit_lse=emit_lse),
        out_shape=out_shape,
        grid=(bs,),
        in_specs=in_specs,
        out_specs=out_specs,
        compiler_params=_cparams(1),
        name=f"step_attn_l{lbuf}",
    )(*args)


def _ssd_kernel(*refs, rows, has_init):
    it = iter(refs)
    z_refs = [next(it) for _ in range(NG_B)]
    xs_refs = [next(it) for _ in range(NG_B)]
    b_ref, c_ref, dt_ref = next(it), next(it), next(it)
    cw_ref, cb_ref, dtb_ref, alog_ref, d_ref, nw_ref = (next(it) for _ in range(6))
    s0_ref, conv0_ref = (next(it), next(it)) if has_init else (None, None)
    y_ref, sout_ref = next(it), next(it)
    state, prev, yscr = next(it), next(it), next(it)
    cl = SSD_CHUNK
    c = pl.program_id(1)

    @pl.when(c == 0)
    def _():
        if has_init:
            state[...] = s0_ref[0]
            prev[...] = conv0_ref[0]
        else:
            state[...] = jnp.zeros_like(state)
            prev[...] = jnp.zeros_like(prev)

    def load(ref):
        u = ref[...]
        if rows < cl:
            u = jnp.concatenate([u, jnp.zeros((cl - rows, u.shape[1]), F32)], axis=0)
        return u

    def conv_silu(u, off):
        w = u.shape[1]
        pv = prev[:, off:off + w]
        cw = cw_ref[:, off:off + w]
        acc = u * cw[CONV_W - 1:CONV_W] + cb_ref[:, off:off + w]
        r8 = lax.broadcasted_iota(jnp.int32, (SUBLANES, w), 0)
        for k in range(1, CONV_W):
            rolled = pltpu.roll(u, k, axis=0)
            top = jnp.where(r8 < k, pltpu.roll(pv, k, axis=0), rolled[:SUBLANES])
            sh = jnp.concatenate([top, rolled[SUBLANES:]], axis=0)
            acc = acc + sh * cw[CONV_W - 1 - k:CONV_W - k]
        if rows == cl:
            prev[:, off:off + w] = u[cl - SUBLANES:]
        return _silu(acc)

    row = lax.broadcasted_iota(jnp.int32, (cl, LANES), 0)
    col = lax.broadcasted_iota(jnp.int32, (cl, LANES), 1)
    x = load(dt_ref) + dtb_ref[...]
    dt = jnp.maximum(x, 0.0) + jnp.log1p(jnp.exp(-jnp.abs(x)))
    if rows < cl:
        dt = jnp.where(row < rows, dt, 0.0)
    acum = dt * (-jnp.exp(alog_ref[...]))
    s = 1
    while s < cl:
        acum = acum + jnp.where(row >= s, pltpu.roll(acum, s, axis=0), 0.0)
        s *= 2
    a_last = acum[cl - 1:cl, :]
    ea = jnp.exp(acum)
    de = jnp.exp(a_last - acum) * dt
    cd = jnp.exp(a_last)
    ac_t, dt_t, de_t = acum.T, dt.T, de.T
    tri = row >= col

    bc = conv_silu(load(b_ref), D_INNER).astype(BF16)
    cc = conv_silu(load(c_ref), D_INNER + NG_B * N_B).astype(BF16)
    nn = (((1,), (0,)), ((), ()))
    nt = (((1,), (1,)), ((), ()))
    for g in range(NG_B):
        bg = bc[:, g * N_B:(g + 1) * N_B]
        cg = cc[:, g * N_B:(g + 1) * N_B]
        cbm = lax.dot_general(cg, bg, nt, preferred_element_type=F32)
        xg = conv_silu(load(xs_refs[g]), g * GW_B)
        xgb = xg.astype(BF16)
        xg_t = xg.T
        for hh in range(HPG_B):
            h = g * HPG_B + hh
            hs = slice(hh * P_B, (hh + 1) * P_B)
            seg = acum[:, h:h + 1] - ac_t[h:h + 1, :]
            dec = jnp.exp(jnp.where(tri, seg, NEG_INF))
            mm = (cbm * dec * dt_t[h:h + 1, :]).astype(BF16)
            sh = state[h]
            yd = lax.dot_general(mm, xgb[:, hs], nn, preferred_element_type=F32)
            yo = lax.dot_general(cg, sh.astype(BF16), nt, preferred_element_type=F32)
            yscr[:, h * P_B:(h + 1) * P_B] = yd + yo * ea[:, h:h + 1] + d_ref[:, h:h + 1] * xg[:, hs]
            xt = (xg_t[hs, :] * de_t[h:h + 1, :]).astype(BF16)
            state[h] = cd[:, h:h + 1] * sh + lax.dot_general(xt, bg, nn, preferred_element_type=F32)
        gs = slice(g * GW_B, (g + 1) * GW_B)
        gt = yscr[:, gs] * _silu(load(z_refs[g]))
        gt = gt * lax.rsqrt(jnp.mean(gt * gt, axis=-1, keepdims=True) + RMS_EPS) * nw_ref[:, gs]
        y_ref[:, gs] = gt[:rows].astype(y_ref.dtype)

    @pl.when(c == pl.num_programs(1) - 1)
    def _():
        sout_ref[0] = state[...]


def _ssd(p, conv_w, conv_b, dt_bias, a_log, d_skip, norm_w, init_state, init_conv, *, row0, nseq, seqlen, out_dtype):
    has_init = init_state is not None
    if seqlen % SSD_CHUNK == 0:
        rows, nc = SSD_CHUNK, seqlen // SSD_CHUNK
    else:
        assert seqlen == SUBLANES
        rows, nc = seqlen, 1
    assert row0 % rows == 0
    r0 = row0 // rows

    def blk(w, start):
        return pl.BlockSpec((rows, w), lambda b, c: (r0 + b * nc + c, start // w))

    def whole(a):
        return pl.BlockSpec(a.shape, lambda b, c: (0,) * a.ndim)

    lane_pad = lambda v: jnp.pad(v.astype(F32), (0, LANES - v.shape[0])).reshape(1, LANES)
    params = [conv_w.astype(F32), conv_b.astype(F32).reshape(1, CONV_DIM), lane_pad(dt_bias), lane_pad(a_log),
              lane_pad(d_skip), norm_w.astype(F32).reshape(1, D_INNER)]
    in_specs = ([blk(GW_B, ABP_Z + g * GW_B) for g in range(NG_B)]
                + [blk(GW_B, ABP_XS + g * GW_B) for g in range(NG_B)]
                + [blk(NG_B * N_B, ABP_B), blk(NG_B * N_B, ABP_C), blk(LANES, ABP_DT)]
                + [whole(a) for a in params])
    args = [p] * (2 * NG_B + 3) + params
    if has_init:
        in_specs += [pl.BlockSpec((1, NH_B, P_B, N_B), lambda b, c: (b, 0, 0, 0)),
                     pl.BlockSpec((1, SUBLANES, CONV_DIM), lambda b, c: (b, 0, 0))]
        args += [init_state.astype(F32),
                 jnp.pad(init_conv.astype(F32), ((0, 0), (SUBLANES - (CONV_W - 1), 0), (0, 0)))]
    return pl.pallas_call(
        functools.partial(_ssd_kernel, rows=rows, has_init=has_init),
        out_shape=[jax.ShapeDtypeStruct((nseq * seqlen, D_INNER), out_dtype),
                   jax.ShapeDtypeStruct((nseq, NH_B, P_B, N_B), F32)],
        grid=(nseq, nc),
        in_specs=in_specs,
        out_specs=[pl.BlockSpec((rows, D_INNER), lambda b, c: (b * nc + c, 0)),
                   pl.BlockSpec((1, NH_B, P_B, N_B), lambda b, c: (b, 0, 0, 0))],
        scratch_shapes=[pltpu.VMEM((NH_B, P_B, N_B), F32), pltpu.VMEM((SUBLANES, CONV_DIM), F32),
                        pltpu.VMEM((SSD_CHUNK, D_INNER), F32)],
        compiler_params=_cparams(2),
        name=f"ssd_r{rows}",
    )(*args)


def _two_source(np_tiles, body, p_refs, s_refs):
    i = pl.program_id(0)

    @pl.when(i < np_tiles)
    def _():
        body(*[r[...] for r in p_refs])

    @pl.when(i >= np_tiles)
    def _():
        body(*[r[...] for r in s_refs])


def _ab_out_kernel(ap_ref, yp_ref, as_ref, ys_ref, wa_ref, wy_ref, x_ref, o_ref, *, np_tiles):
    def body(att, y):
        o_ref[...] = (x_ref[...] + jnp.dot(att.astype(BF16), wa_ref[...], preferred_element_type=F32)
                      + jnp.dot(y.astype(BF16), wy_ref[...], preferred_element_type=F32))

    _two_source(np_tiles, body, (ap_ref, yp_ref), (as_ref, ys_ref))


def _c_out_kernel(*refs, np_tiles):
    p_refs, s_refs = refs[0:6], refs[6:12]
    w_ref, x_ref, o_ref = refs[12:]

    def body(o1, o2, o3, l1, l2, l3):
        m = jnp.maximum(jnp.maximum(l1, l2), l3)
        e1, e2, e3 = jnp.exp(l1 - m), jnp.exp(l2 - m), jnp.exp(l3 - m)
        inv = 1.0 / (e1 + e2 + e3)
        o = (_unpack_heads(e1 * inv, HPG_C) * o1.astype(F32) + _unpack_heads(e2 * inv, HPG_C) * o2.astype(F32)
             + _unpack_heads(e3 * inv, HPG_C) * o3.astype(F32))
        o_ref[...] = x_ref[...] + jnp.dot(o.astype(BF16), w_ref[...], preferred_element_type=F32)

    _two_source(np_tiles, body, p_refs, s_refs)


def _out_proj(kern, prompt_arrs, sample_arrs, weights, x, name):
    t, d = x.shape
    tp, ts = prompt_arrs[0].shape[0], sample_arrs[0].shape[0]
    assert tp + ts == t
    tm = _pick(math.gcd(tp, ts), (512, 256, 128))
    npt = tp // tm
    in_specs = ([pl.BlockSpec((tm, a.shape[1]), lambda i: (jnp.minimum(i, npt - 1), 0)) for a in prompt_arrs]
                + [pl.BlockSpec((tm, a.shape[1]), lambda i: (jnp.maximum(i - npt, 0), 0)) for a in sample_arrs]
                + [pl.BlockSpec(w.shape, lambda i: (0, 0)) for w in weights]
                + [pl.BlockSpec((tm, d), lambda i: (i, 0))])
    return pl.pallas_call(
        functools.partial(kern, np_tiles=npt),
        out_shape=jax.ShapeDtypeStruct((t, d), F32),
        grid=(t // tm,),
        in_specs=in_specs,
        out_specs=pl.BlockSpec((tm, d), lambda i: (i, 0)),
        compiler_params=_cparams(1),
        name=name,
    )(*prompt_arrs, *sample_arrs, *weights, x)


def kernel(x_prompt, x_sample, cache_a_k, cache_a_v, state_b_ssm, state_b_conv, cache_c1_k, cache_c1_v,
           cache_c2_k, cache_c2_v, cache_c3_k, cache_c3_v, rel_bias, norm_ff1, norm_mix, norm_ff2,
           norm_final, ff1_gate, ff1_up, ff1_down, ff2_gate, ff2_up, ff2_down, ab_w_in, ab_w_out, a_sinks,
           b_conv_w, b_conv_b, b_dt_bias, b_a_log, b_d, b_norm_w, c_w_in, c_w_out):
    bp, lp, d = x_prompt.shape
    bs, ls, _ = x_sample.shape
    tp, ts = bp * lp, bs * ls
    c_cache = ((cache_c1_k, cache_c1_v), (cache_c2_k, cache_c2_v), (cache_c3_k, cache_c3_v))
    x = jnp.concatenate([x_prompt.reshape(tp, d), x_sample.reshape(ts, d)], axis=0)

    a_bias_p = _band_bias(rel_bias[:, :HQ_A], 1, WIN_A)
    ab_p, ab_s, c_p, c_s = [], [], [], []
    for layer in range(DEPTH):
        x = _ffn(x, norm_ff1[layer], ff1_gate[layer].astype(BF16), ff1_up[layer].astype(BF16),
                 ff1_down[layer].astype(BF16))
        if layer % 2 == 0:
            e = layer // 2
            w = ab_w_in[e]
            xbc0 = AB_Q + 2 * AB_KV + D_INNER
            w_in = jnp.concatenate([w[:, :AB_Q + 2 * AB_KV], w[:, xbc0 + CONV_DIM:],
                                    jnp.zeros((d, ABP_Z - ABP_DT - NH_B), w.dtype),
                                    w[:, AB_Q + 2 * AB_KV:xbc0], w[:, xbc0:xbc0 + CONV_DIM]], axis=1).astype(BF16)
            p = _rms_matmul(x, norm_mix[layer], w_in)
            ssd_params = (b_conv_w[e], b_conv_b[e], b_dt_bias[e], b_a_log[e], b_d[e], b_norm_w[e])

            att_p = _band_attn(p, a_bias_p, a_sinks[e], nseq=bp, seqlen=lp, dil=1, hq=HQ_A, hkv=HKV_A, dh=DH_A,
                               q_start=ABP_Q, k_start=ABP_K, v_start=ABP_V, emit_lse=False, out_dtype=BF16)
            y_p, ssm_p = _ssd(p, *ssd_params, None, None, row0=0, nseq=bp, seqlen=lp, out_dtype=BF16)

            lbuf = cache_a_k.shape[2]
            bias_c, bias_n = _step_bias(rel_bias[:, :HQ_A], lbuf, ls, 1, WIN_A)
            sink_rows = jnp.repeat(a_sinks[e].astype(F32), ls).reshape(HQ_A * ls, 1)
            att_s, k_s, v_s = _step_attn(p, cache_a_k[e].reshape(bs, lbuf, AB_KV), cache_a_v[e].reshape(bs, lbuf, AB_KV),
                                         bias_c, bias_n, sink_rows, row0=tp, ls=ls, hq=HQ_A, hkv=HKV_A, dh=DH_A,
                                         q_start=ABP_Q, k_start=ABP_K, v_start=ABP_V, emit_lse=False)
            y_s, ssm_s = _ssd(p, *ssd_params, state_b_ssm[e], state_b_conv[e], row0=tp, nseq=bs, seqlen=ls,
                              out_dtype=F32)

            w_out = ab_w_out[e].astype(BF16)
            x = _out_proj(_ab_out_kernel, (att_p, y_p), (att_s, y_s), (w_out[:AB_Q], w_out[AB_Q:]), x, "ab_out")

            keep = min(WIN_A, lp)
            pp = p[:tp].reshape(bp, lp, ABP_N)
            xbc_p = pp[:, :, ABP_XS:]
            xbc_s = p[tp:, ABP_XS:].reshape(bs, ls, CONV_DIM)
            ab_p.append((pp[:, lp - keep:, ABP_K:ABP_V].reshape(bp, keep, HKV_A, DH_A),
                         pp[:, lp - keep:, ABP_V:ABP_DT].reshape(bp, keep, HKV_A, DH_A),
                         ssm_p,
                         jnp.concatenate([jnp.zeros((bp, CONV_W - 1, CONV_DIM), F32), xbc_p[:, lp - (CONV_W - 1):]],
                                         axis=1)[:, -(CONV_W - 1):]))
            ab_s.append((k_s.reshape(bs, lbuf, HKV_A, DH_A), v_s.reshape(bs, lbuf, HKV_A, DH_A), ssm_s,
                         jnp.concatenate([state_b_conv[e], xbc_s], axis=1)[:, -(CONV_W - 1):]))
        else:
            o = layer // 2
            p = _rms_matmul(x, norm_mix[layer], c_w_in[o].astype(BF16))
            pp = p[:tp].reshape(bp, lp, C_IN)
            outs_p, lses_p, outs_s, lses_s, st_p, st_s = [], [], [], [], [], []
            for gi, (win, dil) in enumerate(C_GROUPS):
                n_win = win // dil
                bias_heads = rel_bias[:, gi * HPG_C:(gi + 1) * HPG_C]
                starts = dict(q_start=gi * C_OUT, k_start=(3 + gi) * C_OUT, v_start=(6 + gi) * C_OUT)
                o_p, l_p = _band_attn(p, _band_bias(bias_heads, dil, n_win), None, nseq=bp, seqlen=lp, dil=dil,
                                      hq=HPG_C, hkv=HPG_C, dh=DH_C, emit_lse=True, out_dtype=BF16, **starts)
                ck, cv = c_cache[gi][0][o], c_cache[gi][1][o]
                lbuf = ck.shape[1]
                bias_c, bias_n = _step_bias(bias_heads, lbuf, ls, dil, n_win)
                o_s, l_s, k_s, v_s = _step_attn(p, ck.reshape(bs, lbuf, C_OUT), cv.reshape(bs, lbuf, C_OUT), bias_c,
                                                bias_n, None, row0=tp, ls=ls, hq=HPG_C, hkv=HPG_C, dh=DH_C,
                                                emit_lse=True, **starts)
                outs_p.append(o_p)
                lses_p.append(l_p)
                outs_s.append(o_s)
                lses_s.append(l_s)
                keep = min(win, lp)
                k0, v0 = (3 + gi) * C_OUT, (6 + gi) * C_OUT
                st_p += [pp[:, lp - keep:, k0:k0 + C_OUT].reshape(bp, keep, HPG_C, DH_C),
                         pp[:, lp - keep:, v0:v0 + C_OUT].reshape(bp, keep, HPG_C, DH_C)]
                st_s += [k_s.reshape(bs, lbuf, HPG_C, DH_C), v_s.reshape(bs, lbuf, HPG_C, DH_C)]
            x = _out_proj(_c_out_kernel, (*outs_p, *lses_p), (*outs_s, *lses_s), (c_w_out[o].astype(BF16),), x,
                          "c_out")
            c_p.append(tuple(st_p))
            c_s.append(tuple(st_s))
        x = _ffn(x, norm_ff2[layer], ff2_gate[layer].astype(BF16), ff2_up[layer].astype(BF16),
                 ff2_down[layer].astype(BF16), final_w=norm_final if layer == DEPTH - 1 else None)

    def stack(states):
        return [jnp.stack([s[i] for s in states]) for i in range(len(states[0]))]

    return (x[:tp].reshape(bp, lp, d), x[tp:].reshape(bs, ls, d),
            *stack(ab_p), *stack(c_p), *stack(ab_s), *stack(c_s))
```

```python
import functools
import math

import jax
import jax.numpy as jnp
from jax import lax
from jax.experimental import pallas as pl
from jax.experimental.pallas import tpu as pltpu

F32 = jnp.float32
BF16 = jnp.bfloat16

D_MODEL = 1024
DEPTH = 4
NUM_BUCKETS = 32
MAX_DISTANCE = 2048
Q_BLOCK = 128
RMS_EPS = 1e-6
NEG_INF = -1e30
HQ_A, HKV_A, DH_A, WIN_A = 12, 4, 64, 128
D_INNER = 2 * D_MODEL
P_B = 64
NH_B = D_INNER // P_B
N_B = 128
NG_B = 4
HPG_B = NH_B // NG_B
GW_B = HPG_B * P_B
CONV_W = 4
CONV_DIM = D_INNER + 2 * NG_B * N_B
SSD_CHUNK = 128
C_GROUPS = ((128, 1), (512, 4), (2048, 16))
HPG_C = 4
DH_C = 128
C_OUT = HPG_C * DH_C
AB_Q = HQ_A * DH_A
AB_KV = HKV_A * DH_A
C_IN = 3 * 3 * C_OUT

ABP_Q, ABP_K, ABP_V, ABP_DT = 0, AB_Q, AB_Q + AB_KV, AB_Q + 2 * AB_KV
ABP_Z = ABP_DT + 256
ABP_XS = ABP_Z + D_INNER
ABP_B = ABP_XS + D_INNER
ABP_C = ABP_B + NG_B * N_B
ABP_N = ABP_C + NG_B * N_B

SUBLANES = 8
LANES = 128
VMEM_LIMIT = 56 * 1024 * 1024


def _cparams(n_axes):
    return pltpu.CompilerParams(dimension_semantics=("arbitrary",) * n_axes, vmem_limit_bytes=VMEM_LIMIT)


def _pick(n, options):
    for o in options:
        if n % o == 0:
            return o
    raise ValueError(f"no tile in {options} divides {n}")


def _rms(x, w):
    ms = jnp.mean(x * x, axis=-1, keepdims=True)
    return x * lax.rsqrt(ms + RMS_EPS) * w


def _silu(x):
    return x * jax.nn.sigmoid(x)


def _resident(a):
    return pl.BlockSpec(a.shape, lambda *_: (0,) * a.ndim, pipeline_mode=pl.Buffered(1))


def _ffn_kernel(x_ref, nw_ref, wg_ref, wu_ref, wd_ref, fw_ref, o_ref, *, final_norm):
    x = x_ref[...]
    h = _rms(x, nw_ref[...]).astype(BF16)
    g = jnp.dot(h, wg_ref[...], preferred_element_type=F32)
    u = jnp.dot(h, wu_ref[...], preferred_element_type=F32)
    a = (_silu(g) * u).astype(BF16)
    y = x + 0.5 * jnp.dot(a, wd_ref[...], preferred_element_type=F32)
    if final_norm:
        y = _rms(y, fw_ref[...])
    o_ref[...] = y


def _ffn(x, nw, wg, wu, wd, final_w=None):
    t, d = x.shape
    tm = _pick(t, (512, 256, 128))
    fw = jnp.ones((d,), F32) if final_w is None else final_w
    args = (x, nw.reshape(1, d), wg, wu, wd, fw.reshape(1, d))
    return pl.pallas_call(
        functools.partial(_ffn_kernel, final_norm=final_w is not None),
        out_shape=jax.ShapeDtypeStruct((t, d), F32),
        grid=(t // tm,),
        in_specs=[pl.BlockSpec((tm, d), lambda i: (i, 0))] + [_resident(a) for a in args[1:]],
        out_specs=pl.BlockSpec((tm, d), lambda i: (i, 0)),
        compiler_params=_cparams(1),
        name="ffn",
    )(*args)


def _rms_matmul_kernel(x_ref, nw_ref, w_ref, o_ref):
    h = _rms(x_ref[...], nw_ref[...]).astype(BF16)
    o_ref[...] = jnp.dot(h, w_ref[...], preferred_element_type=F32)


def _rms_matmul(x, nw, w):
    t, d = x.shape
    n = w.shape[1]
    tm = _pick(t, (512, 256, 128))
    args = (x, nw.reshape(1, d), w)
    return pl.pallas_call(
        _rms_matmul_kernel,
        out_shape=jax.ShapeDtypeStruct((t, n), F32),
        grid=(t // tm,),
        in_specs=[pl.BlockSpec((tm, d), lambda i: (i, 0))] + [_resident(a) for a in args[1:]],
        out_specs=pl.BlockSpec((tm, n), lambda i: (i, 0)),
        compiler_params=_cparams(1),
        name="rms_matmul",
    )(*args)


def _t5_bucket(dist):
    n = jnp.maximum(dist, 0)
    max_exact = NUM_BUCKETS // 2
    nf = jnp.maximum(n, 1).astype(F32)
    large = max_exact + (jnp.log(nf / max_exact) / math.log(MAX_DISTANCE / max_exact)
                         * (NUM_BUCKETS - max_exact)).astype(jnp.int32)
    large = jnp.minimum(large, NUM_BUCKETS - 1)
    return jnp.where(n < max_exact, n, large)


def _bias_lookup(bias_heads, dist):
    onehot = jax.nn.one_hot(_t5_bucket(dist), NUM_BUCKETS, dtype=F32)
    return jnp.einsum('...k,kh->...h', onehot, bias_heads.astype(F32), precision=lax.Precision.HIGHEST)


def _band_bias(bias_heads, dil, n_win):
    qi = jnp.arange(Q_BLOCK)[:, None]
    si = jnp.arange(2 * Q_BLOCK)[None, :]
    dist = qi + Q_BLOCK - si
    valid = (dist >= 0) & (dist <= n_win)
    bias = jnp.transpose(_bias_lookup(bias_heads, dist * dil), (2, 0, 1))
    return jnp.where(valid[None], bias, NEG_INF)


def _step_bias(bias_heads, lbuf, ls, dil, n_win, head_rows):
    nh = bias_heads.shape[1]
    i = jnp.arange(ls)[:, None]
    eye = jnp.eye(nh, dtype=bool)

    def table(delta, head_major):
        ok = (delta >= 0) & (delta % dil == 0) & (delta // dil <= n_win)
        b = jnp.transpose(jnp.where(ok[..., None], _bias_lookup(bias_heads, delta), NEG_INF), (2, 0, 1))
        if head_rows and head_major:
            b = jnp.where(eye[:, None, :, None], b[:, :, None, :], NEG_INF)
        elif head_rows:
            b = jnp.where(eye[:, None, None, :], b[:, :, :, None], NEG_INF)
        return b.reshape(nh * ls, -1)

    bias_c = table(lbuf + i - jnp.arange(lbuf)[None, :], False)
    bias_n = table(i - jnp.arange(ls)[None, :], True)
    bias_n = jnp.pad(bias_n, ((0, 0), (0, LANES - bias_n.shape[1])), constant_values=NEG_INF)
    return bias_c, bias_n


def _band_attn_kernel(*refs, hq, hkv, dh, has_sink, emit_lse):
    it = iter(refs)
    q_ref, kp_ref, kc_ref, vp_ref, vc_ref, bias_ref = (next(it) for _ in range(6))
    sink_ref = next(it) if has_sink else None
    o_ref = next(it)
    lse_ref = next(it) if emit_lse else None
    g = hq // hkv
    scale = dh ** -0.5
    col = lax.broadcasted_iota(jnp.int32, (Q_BLOCK, 2 * Q_BLOCK), 1)
    no_prev = col < jnp.where(pl.program_id(2) == 0, Q_BLOCK, 0)
    outs, lses = [], []
    for kv in range(hkv):
        sl = slice(kv * dh, (kv + 1) * dh)
        k = jnp.concatenate([kp_ref[:, sl], kc_ref[:, sl]], axis=0).astype(BF16)
        v = jnp.concatenate([vp_ref[:, sl], vc_ref[:, sl]], axis=0).astype(BF16)
        for gi in range(g):
            h = kv * g + gi
            qh = (q_ref[:, h * dh:(h + 1) * dh] * scale).astype(BF16)
            s = lax.dot_general(qh, k, (((1,), (1,)), ((), ())), preferred_element_type=F32)
            s = jnp.where(no_prev, NEG_INF, s + bias_ref[h])
            m = jnp.max(s, axis=-1, keepdims=True)
            if has_sink:
                m = jnp.maximum(m, sink_ref[h])
            p = jnp.exp(s - m)
            l = jnp.sum(p, axis=-1, keepdims=True)
            if has_sink:
                l = l + jnp.exp(sink_ref[h] - m)
            o = jnp.dot(p.astype(BF16), v, preferred_element_type=F32)
            outs.append(o * (1.0 / l))
            lses.append(m + jnp.log(l))
    o_ref[...] = jnp.concatenate(outs, axis=1).astype(o_ref.dtype)
    if emit_lse:
        lse_ref[...] = _pack_heads(lses)


def _pack_heads(cols):
    n = len(cols)
    rows = cols[0].shape[0]
    lane = lax.broadcasted_iota(jnp.int32, (rows, LANES), 1)
    out = jnp.broadcast_to(cols[-1], (rows, LANES))
    for h in range(n - 2, -1, -1):
        out = jnp.where(lane < (h + 1) * (LANES // n), cols[h], out)
    return out


def _unpack_heads(packed, n):
    rows = packed.shape[0]
    w = LANES // n
    return jnp.concatenate([jnp.broadcast_to(packed[:, h * w:h * w + 1], (rows, LANES)) for h in range(n)], axis=1)


def _band_attn(p, bias, sinks, *, nseq, seqlen, dil, hq, hkv, dh, q_start, k_start, v_start, emit_lse, out_dtype):
    t, ncols = p.shape
    wq, wk = hq * dh, hkv * dh
    assert seqlen % (dil * Q_BLOCK) == 0 and t % dil == 0
    assert q_start % wq == 0 and k_start % wk == 0 and v_start % wk == 0
    assert dil == 1 or (ncols % wq == 0 and ncols % wk == 0)
    nbd = seqlen // dil // Q_BLOCK
    pv = p.reshape(t // dil, dil * ncols)
    tq = nseq * seqlen // dil

    def cur(start, w):
        return lambda b, r, n: (b * nbd + n, r * (ncols // w) + start // w)

    def prev(start, w):
        return lambda b, r, n: (b * nbd + jnp.maximum(n - 1, 0), r * (ncols // w) + start // w)

    in_specs = [
        pl.BlockSpec((Q_BLOCK, wq), cur(q_start, wq)),
        pl.BlockSpec((Q_BLOCK, wk), prev(k_start, wk)),
        pl.BlockSpec((Q_BLOCK, wk), cur(k_start, wk)),
        pl.BlockSpec((Q_BLOCK, wk), prev(v_start, wk)),
        pl.BlockSpec((Q_BLOCK, wk), cur(v_start, wk)),
        pl.BlockSpec((hq, Q_BLOCK, 2 * Q_BLOCK), lambda b, r, n: (0, 0, 0)),
    ]
    args = [pv, pv, pv, pv, pv, bias]
    if sinks is not None:
        in_specs.append(pl.BlockSpec(memory_space=pltpu.SMEM))
        args.append(sinks.astype(F32))
    out_shape = [jax.ShapeDtypeStruct((tq, dil * wq), out_dtype)]
    out_specs = [pl.BlockSpec((Q_BLOCK, wq), lambda b, r, n: (b * nbd + n, r))]
    if emit_lse:
        out_shape.append(jax.ShapeDtypeStruct((tq, dil * LANES), F32))
        out_specs.append(pl.BlockSpec((Q_BLOCK, LANES), lambda b, r, n: (b * nbd + n, r)))
    res = pl.pallas_call(
        functools.partial(_band_attn_kernel, hq=hq, hkv=hkv, dh=dh, has_sink=sinks is not None, emit_lse=emit_lse),
        out_shape=out_shape,
        grid=(nseq, dil, nbd),
        in_specs=in_specs,
        out_specs=out_specs,
        compiler_params=_cparams(3),
        name=f"band_attn_d{dil}",
    )(*args)
    o = res[0].reshape(nseq * seqlen, wq)
    if emit_lse:
        return o, res[1].reshape(nseq * seqlen, LANES)
    return o


def _step_attn_kernel(*refs, ls, hq, hkv, dh, has_sink, emit_lse):
    it = iter(refs)
    q_ref, kn_ref, vn_ref, ck_ref, cv_ref, bc_ref, bn_ref = (next(it) for _ in range(7))
    sink_ref = next(it) if has_sink else None
    o_ref = next(it)
    lse_ref = next(it) if emit_lse else None
    ok_ref, ov_ref = next(it), next(it)
    lbuf = ck_ref.shape[1]
    w = hkv * dh
    g = hq // hkv
    kn, vn = kn_ref[...], vn_ref[...]

    ok_ref[0, :lbuf - ls, :] = ck_ref[0, ls:, :]
    ok_ref[0, lbuf - ls:, :] = kn
    ov_ref[0, :lbuf - ls, :] = cv_ref[0, ls:, :]
    ov_ref[0, lbuf - ls:, :] = vn

    q = q_ref[...] * (dh ** -0.5)
    if g == 1:
        lane = lax.broadcasted_iota(jnp.int32, (ls, w), 1)
        head_of_lane = [(lane >= h * dh) & (lane < (h + 1) * dh) for h in range(hq)]
        qbd = jnp.concatenate([jnp.where(head_of_lane[h], q, 0.0) for h in range(hq)], axis=0)
    else:
        blocks = []
        for h in range(hq):
            kv = h // g
            parts = []
            if kv > 0:
                parts.append(jnp.zeros((ls, kv * dh), F32))
            parts.append(q[:, h * dh:(h + 1) * dh])
            if kv < hkv - 1:
                parts.append(jnp.zeros((ls, (hkv - 1 - kv) * dh), F32))
            blocks.append(jnp.concatenate(parts, axis=1))
        qbd = jnp.concatenate(blocks, axis=0)
    qbd = qbd.astype(BF16)

    pad = jnp.zeros((LANES - ls, w), F32)
    knp = jnp.concatenate([kn, pad], axis=0).astype(BF16)
    vnp = jnp.concatenate([vn, pad], axis=0).astype(BF16)
    nt = (((1,), (1,)), ((), ()))
    sc = lax.dot_general(qbd, ck_ref[0].astype(BF16), nt, preferred_element_type=F32) + bc_ref[...]
    sn = lax.dot_general(qbd, knp, nt, preferred_element_type=F32) + bn_ref[...]
    m = jnp.maximum(jnp.max(sc, axis=-1, keepdims=True), jnp.max(sn, axis=-1, keepdims=True))
    if has_sink:
        m = jnp.maximum(m, sink_ref[...])
    pc = jnp.exp(sc - m)
    pn = jnp.exp(sn - m)
    l = jnp.sum(pc, axis=-1, keepdims=True) + jnp.sum(pn, axis=-1, keepdims=True)
    if has_sink:
        l = l + jnp.exp(sink_ref[...] - m)
    of = (jnp.dot(pc.astype(BF16), cv_ref[0].astype(BF16), preferred_element_type=F32)
          + jnp.dot(pn.astype(BF16), vnp, preferred_element_type=F32)) * (1.0 / l)

    if g == 1:
        o = jnp.where(head_of_lane[0], of[:ls], 0.0)
        for h in range(1, hq):
            o = o + jnp.where(head_of_lane[h], of[h * ls:(h + 1) * ls], 0.0)
    else:
        o = jnp.concatenate([of[h * ls:(h + 1) * ls, (h // g) * dh:(h // g + 1) * dh] for h in range(hq)], axis=1)
    o_ref[...] = o
    if emit_lse:
        lse = m + jnp.log(l)
        lse_ref[...] = _pack_heads([lse[h * ls:(h + 1) * ls] for h in range(hq)])


def _step_attn(p, cache_k, cache_v, bias_c, bias_n, sink_rows, *, row0, ls, hq, hkv, dh, q_start, k_start, v_start,
               emit_lse):
    bs, lbuf, w = cache_k.shape
    wq = hq * dh
    assert ls == SUBLANES and row0 % ls == 0 and w == hkv * dh and lbuf % ls == 0
    assert q_start % wq == 0 and k_start % w == 0 and v_start % w == 0
    r0 = row0 // ls
    in_specs = [
        pl.BlockSpec((ls, wq), lambda b: (r0 + b, q_start // wq)),
        pl.BlockSpec((ls, w), lambda b: (r0 + b, k_start // w)),
        pl.BlockSpec((ls, w), lambda b: (r0 + b, v_start // w)),
        pl.BlockSpec((1, lbuf, w), lambda b: (b, 0, 0)),
        pl.BlockSpec((1, lbuf, w), lambda b: (b, 0, 0)),
        pl.BlockSpec((hq * ls, lbuf), lambda b: (0, 0)),
        pl.BlockSpec((hq * ls, LANES), lambda b: (0, 0)),
    ]
    args = [p, p, p, cache_k, cache_v, bias_c, bias_n]
    if sink_rows is not None:
        in_specs.append(pl.BlockSpec((hq * ls, 1), lambda b: (0, 0)))
        args.append(sink_rows)
    out_shape = [jax.ShapeDtypeStruct((bs * ls, wq), F32)]
    out_specs = [pl.BlockSpec((ls, wq), lambda b: (b, 0))]
    if emit_lse:
        out_shape.append(jax.ShapeDtypeStruct((bs * ls, LANES), F32))
        out_specs.append(pl.BlockSpec((ls, LANES), lambda b: (b, 0)))
    out_shape += [jax.ShapeDtypeStruct((bs, lbuf, w), F32)] * 2
    out_specs += [pl.BlockSpec((1, lbuf, w), lambda b: (b, 0, 0))] * 2
    return pl.pallas_call(
        functools.partial(_step_attn_kernel, ls=ls, hq=hq, hkv=hkv, dh=dh, has_sink=sink_rows is not None,
                          emit_lse=emit_lse),
        out_shape=out_shape,
        grid=(bs,),
        in_specs=in_specs,
        out_specs=out_specs,
        compiler_params=_cparams(1),
        name=f"step_attn_l{lbuf}",
    )(*args)


def _step_attn_rows_kernel(*refs, ls, nh, dh, aliased):
    it = iter(refs)
    q_ref, kn_ref, vn_ref, ck_ref, cv_ref, bc_ref, bn_ref = (next(it) for _ in range(7))
    if aliased:
        next(it), next(it)
    o_ref, lse_ref, ok_ref, ov_ref = (next(it) for _ in range(4))
    rows = ck_ref.shape[0]
    nnew = ls * nh
    kn, vn = kn_ref[...], vn_ref[...]
    heads = [slice(h * dh, (h + 1) * dh) for h in range(nh)]

    ok_ref[:rows - nnew, :] = ck_ref[nnew:, :]
    ov_ref[:rows - nnew, :] = cv_ref[nnew:, :]
    for i in range(ls):
        for h in range(nh):
            r = rows - nnew + i * nh + h
            ok_ref[r:r + 1, :] = kn[i:i + 1, heads[h]]
            ov_ref[r:r + 1, :] = vn[i:i + 1, heads[h]]

    q = q_ref[...] * (dh ** -0.5)
    qr = jnp.concatenate([q[:, hs] for hs in heads], axis=0).astype(BF16)
    pad = [jnp.zeros((LANES - nnew, dh), F32)]
    knr = jnp.concatenate([kn[:, hs] for hs in heads] + pad, axis=0).astype(BF16)
    vnr = jnp.concatenate([vn[:, hs] for hs in heads] + pad, axis=0).astype(BF16)
    nt = (((1,), (1,)), ((), ()))
    sc = lax.dot_general(qr, ck_ref[...].astype(BF16), nt, preferred_element_type=F32) + bc_ref[...]
    sn = lax.dot_general(qr, knr, nt, preferred_element_type=F32) + bn_ref[...]
    m = jnp.maximum(jnp.max(sc, axis=-1, keepdims=True), jnp.max(sn, axis=-1, keepdims=True))
    pc = jnp.exp(sc - m)
    pn = jnp.exp(sn - m)
    l = jnp.sum(pc, axis=-1, keepdims=True) + jnp.sum(pn, axis=-1, keepdims=True)
    of = (jnp.dot(pc.astype(BF16), cv_ref[...].astype(BF16), preferred_element_type=F32)
          + jnp.dot(pn.astype(BF16), vnr, preferred_element_type=F32)) * (1.0 / l)
    o_ref[...] = jnp.concatenate([of[h * ls:(h + 1) * ls] for h in range(nh)], axis=1)
    lse = m + jnp.log(l)
    lse_ref[...] = _pack_heads([lse[h * ls:(h + 1) * ls] for h in range(nh)])


def _step_attn_rows(p, cache_k, cache_v, layer, prev_out, bias_c, bias_n, *, row0, ls, nh, dh, q_start, k_start,
                    v_start):
    nl, bs, rows, _ = cache_k.shape
    w = nh * dh
    assert ls == SUBLANES and row0 % ls == 0 and dh == LANES and rows % (ls * nh) == 0
    assert q_start % w == 0 and k_start % w == 0 and v_start % w == 0
    r0 = row0 // ls
    cache_spec = pl.BlockSpec((None, None, rows, dh), lambda b: (layer, b, 0, 0))
    in_specs = [
        pl.BlockSpec((ls, w), lambda b: (r0 + b, q_start // w)),
        pl.BlockSpec((ls, w), lambda b: (r0 + b, k_start // w)),
        pl.BlockSpec((ls, w), lambda b: (r0 + b, v_start // w)),
        cache_spec, cache_spec,
        pl.BlockSpec((nh * ls, rows), lambda b: (0, 0)),
        pl.BlockSpec((nh * ls, LANES), lambda b: (0, 0)),
    ]
    args = [p, p, p, cache_k, cache_v, bias_c, bias_n]
    aliases = {}
    if prev_out is not None:
        in_specs += [pl.BlockSpec(memory_space=pl.ANY)] * 2
        args += list(prev_out)
        aliases = {7: 2, 8: 3}
    return pl.pallas_call(
        functools.partial(_step_attn_rows_kernel, ls=ls, nh=nh, dh=dh, aliased=prev_out is not None),
        out_shape=[jax.ShapeDtypeStruct((bs * ls, w), F32), jax.ShapeDtypeStruct((bs * ls, LANES), F32),
                   jax.ShapeDtypeStruct(cache_k.shape, F32), jax.ShapeDtypeStruct(cache_v.shape, F32)],
        grid=(bs,),
        in_specs=in_specs,
        out_specs=[pl.BlockSpec((ls, w), lambda b: (b, 0)), pl.BlockSpec((ls, LANES), lambda b: (b, 0)),
                   cache_spec, cache_spec],
        input_output_aliases=aliases,
        compiler_params=_cparams(1),
        name=f"step_attn_rows{rows}",
    )(*args)


def _ssd_kernel(*refs, rows, has_init):
    it = iter(refs)
    z_refs = [next(it) for _ in range(NG_B)]
    xs_refs = [next(it) for _ in range(NG_B)]
    b_ref, c_ref, dt_ref = next(it), next(it), next(it)
    cw_ref, cb_ref, dtb_ref, alog_ref, d_ref, nw_ref = (next(it) for _ in range(6))
    s0_ref, conv0_ref = (next(it), next(it)) if has_init else (None, None)
    y_ref, sout_ref, state, prev, yscr = refs[-5:]
    cl = SSD_CHUNK
    c = pl.program_id(1)

    @pl.when(c == 0)
    def _():
        if has_init:
            state[...] = s0_ref[...]
            prev[...] = conv0_ref[...]
        else:
            state[...] = jnp.zeros_like(state)
            prev[...] = jnp.zeros_like(prev)

    def load(ref):
        u = ref[...]
        if rows < cl:
            u = jnp.concatenate([u, jnp.zeros((cl - rows, u.shape[1]), F32)], axis=0)
        return u

    def conv_silu(u, off):
        w = u.shape[1]
        pv = prev[:, off:off + w]
        cw = cw_ref[:, off:off + w]
        acc = u * cw[CONV_W - 1:CONV_W] + cb_ref[:, off:off + w]
        r8 = lax.broadcasted_iota(jnp.int32, (SUBLANES, w), 0)
        for k in range(1, CONV_W):
            rolled = pltpu.roll(u, k, axis=0)
            top = jnp.where(r8 < k, pltpu.roll(pv, k, axis=0), rolled[:SUBLANES])
            sh = jnp.concatenate([top, rolled[SUBLANES:]], axis=0)
            acc = acc + sh * cw[CONV_W - 1 - k:CONV_W - k]
        if rows == cl:
            prev[:, off:off + w] = u[cl - SUBLANES:]
        return _silu(acc)

    row = lax.broadcasted_iota(jnp.int32, (cl, LANES), 0)
    col = lax.broadcasted_iota(jnp.int32, (cl, LANES), 1)
    x = load(dt_ref) + dtb_ref[...]
    dt = jnp.maximum(x, 0.0) + jnp.log1p(jnp.exp(-jnp.abs(x)))
    if rows < cl:
        dt = jnp.where(row < rows, dt, 0.0)
    acum = dt * (-jnp.exp(alog_ref[...]))
    s = 1
    while s < cl:
        acum = acum + jnp.where(row >= s, pltpu.roll(acum, s, axis=0), 0.0)
        s *= 2
    a_last = acum[cl - 1:cl, :]
    ea = jnp.exp(acum)
    de = jnp.exp(a_last - acum) * dt
    cd = jnp.exp(a_last)
    ac_t, dt_t, de_t = acum.T, dt.T, de.T
    tri = row >= col

    bc = conv_silu(load(b_ref), D_INNER).astype(BF16)
    cc = conv_silu(load(c_ref), D_INNER + NG_B * N_B).astype(BF16)
    nn = (((1,), (0,)), ((), ()))
    nt = (((1,), (1,)), ((), ()))
    for g in range(NG_B):
        bg = bc[:, g * N_B:(g + 1) * N_B]
        cg = cc[:, g * N_B:(g + 1) * N_B]
        cbm = lax.dot_general(cg, bg, nt, preferred_element_type=F32)
        xg = conv_silu(load(xs_refs[g]), g * GW_B)
        xgb = xg.astype(BF16)
        xg_t = xg.T
        for hh in range(HPG_B):
            h = g * HPG_B + hh
            hs = slice(hh * P_B, (hh + 1) * P_B)
            seg = acum[:, h:h + 1] - ac_t[h:h + 1, :]
            dec = jnp.exp(jnp.where(tri, seg, NEG_INF))
            mm = (cbm * dec * dt_t[h:h + 1, :]).astype(BF16)
            sh = state[h]
            yd = lax.dot_general(mm, xgb[:, hs], nn, preferred_element_type=F32)
            yo = lax.dot_general(cg, sh.astype(BF16), nt, preferred_element_type=F32)
            yscr[:, h * P_B:(h + 1) * P_B] = yd + yo * ea[:, h:h + 1] + d_ref[:, h:h + 1] * xg[:, hs]
            xt = (xg_t[hs, :] * de_t[h:h + 1, :]).astype(BF16)
            state[h] = cd[:, h:h + 1] * sh + lax.dot_general(xt, bg, nn, preferred_element_type=F32)
        gs = slice(g * GW_B, (g + 1) * GW_B)
        gt = yscr[:, gs] * _silu(load(z_refs[g]))
        gt = gt * lax.rsqrt(jnp.mean(gt * gt, axis=-1, keepdims=True) + RMS_EPS) * nw_ref[:, gs]
        y_ref[:, gs] = gt[:rows].astype(y_ref.dtype)

    @pl.when(c == pl.num_programs(1) - 1)
    def _():
        sout_ref[...] = state[...]


def _ssd(p, conv_w, conv_b, dt_bias, a_log, d_skip, norm_w, init_state, init_conv, layer=0, prev_out=None, *, row0,
         nseq, seqlen, out_dtype):
    has_init = init_state is not None
    if seqlen % SSD_CHUNK == 0:
        rows, nc = SSD_CHUNK, seqlen // SSD_CHUNK
    else:
        assert seqlen == SUBLANES
        rows, nc = seqlen, 1
    assert row0 % rows == 0
    r0 = row0 // rows

    def blk(w, start):
        return pl.BlockSpec((rows, w), lambda b, c: (r0 + b * nc + c, start // w))

    def whole(a):
        return pl.BlockSpec(a.shape, lambda b, c: (0,) * a.ndim)

    lane_pad = lambda v: jnp.pad(v.astype(F32), (0, LANES - v.shape[0])).reshape(1, LANES)
    params = [conv_w.astype(F32), conv_b.astype(F32).reshape(1, CONV_DIM), lane_pad(dt_bias), lane_pad(a_log),
              lane_pad(d_skip), norm_w.astype(F32).reshape(1, D_INNER)]
    in_specs = ([blk(GW_B, ABP_Z + g * GW_B) for g in range(NG_B)]
                + [blk(GW_B, ABP_XS + g * GW_B) for g in range(NG_B)]
                + [blk(NG_B * N_B, ABP_B), blk(NG_B * N_B, ABP_C), blk(LANES, ABP_DT)]
                + [whole(a) for a in params])
    args = [p] * (2 * NG_B + 3) + params
    aliases = {}
    if has_init:
        state_spec = pl.BlockSpec((None, None, NH_B, P_B, N_B), lambda b, c: (layer, b, 0, 0, 0))
        state_shape = init_state.shape
        in_specs += [state_spec, pl.BlockSpec((None, SUBLANES, CONV_DIM), lambda b, c: (b, 0, 0))]
        args += [init_state, jnp.pad(init_conv.astype(F32), ((0, 0), (SUBLANES - (CONV_W - 1), 0), (0, 0)))]
        if prev_out is not None:
            aliases = {len(args): 1}
            in_specs.append(pl.BlockSpec(memory_space=pl.ANY))
            args.append(prev_out)
    else:
        state_spec = pl.BlockSpec((None, NH_B, P_B, N_B), lambda b, c: (b, 0, 0, 0))
        state_shape = (nseq, NH_B, P_B, N_B)
    return pl.pallas_call(
        functools.partial(_ssd_kernel, rows=rows, has_init=has_init),
        out_shape=[jax.ShapeDtypeStruct((nseq * seqlen, D_INNER), out_dtype),
                   jax.ShapeDtypeStruct(state_shape, F32)],
        grid=(nseq, nc),
        in_specs=in_specs,
        out_specs=[pl.BlockSpec((rows, D_INNER), lambda b, c: (b * nc + c, 0)), state_spec],
        scratch_shapes=[pltpu.VMEM((NH_B, P_B, N_B), F32), pltpu.VMEM((SUBLANES, CONV_DIM), F32),
                        pltpu.VMEM((SSD_CHUNK, D_INNER), F32)],
        input_output_aliases=aliases,
        compiler_params=_cparams(2),
        name=f"ssd_r{rows}",
    )(*args)


def _two_source(np_tiles, body, p_refs, s_refs):
    i = pl.program_id(0)

    @pl.when(i < np_tiles)
    def _():
        body(*[r[...] for r in p_refs])

    @pl.when(i >= np_tiles)
    def _():
        body(*[r[...] for r in s_refs])


def _ab_out_kernel(ap_ref, yp_ref, as_ref, ys_ref, wa_ref, wy_ref, x_ref, o_ref, *, np_tiles):
    def body(att, y):
        o_ref[...] = (x_ref[...] + jnp.dot(att.astype(BF16), wa_ref[...], preferred_element_type=F32)
                      + jnp.dot(y.astype(BF16), wy_ref[...], preferred_element_type=F32))

    _two_source(np_tiles, body, (ap_ref, yp_ref), (as_ref, ys_ref))


def _c_out_kernel(*refs, np_tiles):
    p_refs, s_refs = refs[0:6], refs[6:12]
    w_ref, x_ref, o_ref = refs[12:]

    def body(o1, o2, o3, l1, l2, l3):
        m = jnp.maximum(jnp.maximum(l1, l2), l3)
        e1, e2, e3 = jnp.exp(l1 - m), jnp.exp(l2 - m), jnp.exp(l3 - m)
        inv = 1.0 / (e1 + e2 + e3)
        o = (_unpack_heads(e1 * inv, HPG_C) * o1.astype(F32) + _unpack_heads(e2 * inv, HPG_C) * o2.astype(F32)
             + _unpack_heads(e3 * inv, HPG_C) * o3.astype(F32))
        o_ref[...] = x_ref[...] + jnp.dot(o.astype(BF16), w_ref[...], preferred_element_type=F32)

    _two_source(np_tiles, body, p_refs, s_refs)


def _out_proj(kern, prompt_arrs, sample_arrs, weights, x, name):
    t, d = x.shape
    tp, ts = prompt_arrs[0].shape[0], sample_arrs[0].shape[0]
    assert tp + ts == t
    tm = _pick(math.gcd(tp, ts), (512, 256, 128))
    npt = tp // tm
    in_specs = ([pl.BlockSpec((tm, a.shape[1]), lambda i: (jnp.minimum(i, npt - 1), 0)) for a in prompt_arrs]
                + [pl.BlockSpec((tm, a.shape[1]), lambda i: (jnp.maximum(i - npt, 0), 0)) for a in sample_arrs]
                + [pl.BlockSpec(w.shape, lambda i: (0, 0)) for w in weights]
                + [pl.BlockSpec((tm, d), lambda i: (i, 0))])
    return pl.pallas_call(
        functools.partial(kern, np_tiles=npt),
        out_shape=jax.ShapeDtypeStruct((t, d), F32),
        grid=(t // tm,),
        in_specs=in_specs,
        out_specs=pl.BlockSpec((tm, d), lambda i: (i, 0)),
        compiler_params=_cparams(1),
        name=name,
    )(*prompt_arrs, *sample_arrs, *weights, x)


def kernel(x_prompt, x_sample, cache_a_k, cache_a_v, state_b_ssm, state_b_conv, cache_c1_k, cache_c1_v,
           cache_c2_k, cache_c2_v, cache_c3_k, cache_c3_v, rel_bias, norm_ff1, norm_mix, norm_ff2,
           norm_final, ff1_gate, ff1_up, ff1_down, ff2_gate, ff2_up, ff2_down, ab_w_in, ab_w_out, a_sinks,
           b_conv_w, b_conv_b, b_dt_bias, b_a_log, b_d, b_norm_w, c_w_in, c_w_out):
    bp, lp, d = x_prompt.shape
    bs, ls, _ = x_sample.shape
    tp, ts = bp * lp, bs * ls
    c_cache = ((cache_c1_k, cache_c1_v), (cache_c2_k, cache_c2_v), (cache_c3_k, cache_c3_v))
    x = jnp.concatenate([x_prompt.reshape(tp, d), x_sample.reshape(ts, d)], axis=0)

    def tail_rows(p, keep, c0, w):
        return jnp.stack([lax.slice(p, (s * lp + lp - keep, c0), ((s + 1) * lp, c0 + w)) for s in range(bp)])

    a_bias_p = _band_bias(rel_bias[:, :HQ_A], 1, WIN_A)
    ab_p, ab_s, c_p = [], [], []
    ssm_s = None
    c_rows = [[c.reshape(c.shape[0], bs, c.shape[2] * HPG_C, DH_C) for c in kv] for kv in c_cache]
    c_new = [None] * len(C_GROUPS)
    for layer in range(DEPTH):
        x = _ffn(x, norm_ff1[layer], ff1_gate[layer].astype(BF16), ff1_up[layer].astype(BF16),
                 ff1_down[layer].astype(BF16))
        if layer % 2 == 0:
            e = layer // 2
            w = ab_w_in[e]
            xbc0 = AB_Q + 2 * AB_KV + D_INNER
            w_in = jnp.concatenate([w[:, :AB_Q + 2 * AB_KV], w[:, xbc0 + CONV_DIM:],
                                    jnp.zeros((d, ABP_Z - ABP_DT - NH_B), w.dtype),
                                    w[:, AB_Q + 2 * AB_KV:xbc0], w[:, xbc0:xbc0 + CONV_DIM]], axis=1).astype(BF16)
            p = _rms_matmul(x, norm_mix[layer], w_in)
            ssd_params = (b_conv_w[e], b_conv_b[e], b_dt_bias[e], b_a_log[e], b_d[e], b_norm_w[e])

            att_p = _band_attn(p, a_bias_p, a_sinks[e], nseq=bp, seqlen=lp, dil=1, hq=HQ_A, hkv=HKV_A, dh=DH_A,
                               q_start=ABP_Q, k_start=ABP_K, v_start=ABP_V, emit_lse=False, out_dtype=BF16)
            y_p, ssm_p = _ssd(p, *ssd_params, None, None, row0=0, nseq=bp, seqlen=lp, out_dtype=BF16)

            lbuf = cache_a_k.shape[2]
            bias_c, bias_n = _step_bias(rel_bias[:, :HQ_A], lbuf, ls, 1, WIN_A, head_rows=False)
            sink_rows = jnp.repeat(a_sinks[e].astype(F32), ls).reshape(HQ_A * ls, 1)
            att_s, k_s, v_s = _step_attn(p, cache_a_k[e].reshape(bs, lbuf, AB_KV), cache_a_v[e].reshape(bs, lbuf, AB_KV),
                                         bias_c, bias_n, sink_rows, row0=tp, ls=ls, hq=HQ_A, hkv=HKV_A, dh=DH_A,
                                         q_start=ABP_Q, k_start=ABP_K, v_start=ABP_V, emit_lse=False)
            y_s, ssm_s = _ssd(p, *ssd_params, state_b_ssm, state_b_conv[e], e, ssm_s, row0=tp, nseq=bs, seqlen=ls,
                              out_dtype=F32)

            w_out = ab_w_out[e].astype(BF16)
            x = _out_proj(_ab_out_kernel, (att_p, y_p), (att_s, y_s), (w_out[:AB_Q], w_out[AB_Q:]), x, "ab_out")

            keep = min(WIN_A, lp)
            xbc_p = tail_rows(p, min(CONV_W - 1, lp), ABP_XS, CONV_DIM)
            xbc_s = p[tp:, ABP_XS:].reshape(bs, ls, CONV_DIM)
            ab_p.append((tail_rows(p, keep, ABP_K, AB_KV).reshape(bp, keep, HKV_A, DH_A),
                         tail_rows(p, keep, ABP_V, AB_KV).reshape(bp, keep, HKV_A, DH_A),
                         ssm_p,
                         jnp.concatenate([jnp.zeros((bp, CONV_W - 1, CONV_DIM), F32), xbc_p],
                                         axis=1)[:, -(CONV_W - 1):]))
            ab_s.append((k_s.reshape(bs, lbuf, HKV_A, DH_A), v_s.reshape(bs, lbuf, HKV_A, DH_A),
                         jnp.concatenate([state_b_conv[e], xbc_s], axis=1)[:, -(CONV_W - 1):]))
        else:
            o = layer // 2
            p = _rms_matmul(x, norm_mix[layer], c_w_in[o].astype(BF16))
            outs_p, lses_p, outs_s, lses_s, st_p = [], [], [], [], []
            for gi, (win, dil) in enumerate(C_GROUPS):
                n_win = win // dil
                bias_heads = rel_bias[:, gi * HPG_C:(gi + 1) * HPG_C]
                starts = dict(q_start=gi * C_OUT, k_start=(3 + gi) * C_OUT, v_start=(6 + gi) * C_OUT)
                o_p, l_p = _band_attn(p, _band_bias(bias_heads, dil, n_win), None, nseq=bp, seqlen=lp, dil=dil,
                                      hq=HPG_C, hkv=HPG_C, dh=DH_C, emit_lse=True, out_dtype=BF16, **starts)
                lbuf = c_cache[gi][0].shape[2]
                bias_c, bias_n = _step_bias(bias_heads, lbuf, ls, dil, n_win, head_rows=True)
                o_s, l_s, k_s, v_s = _step_attn_rows(p, c_rows[gi][0], c_rows[gi][1], o, c_new[gi], bias_c, bias_n,
                                                     row0=tp, ls=ls, nh=HPG_C, dh=DH_C, **starts)
                c_new[gi] = (k_s, v_s)
                outs_p.append(o_p)
                lses_p.append(l_p)
                outs_s.append(o_s)
                lses_s.append(l_s)
                keep = min(win, lp)
                st_p += [tail_rows(p, keep, starts["k_start"], C_OUT).reshape(bp, keep, HPG_C, DH_C),
                         tail_rows(p, keep, starts["v_start"], C_OUT).reshape(bp, keep, HPG_C, DH_C)]
            x = _out_proj(_c_out_kernel, (*outs_p, *lses_p), (*outs_s, *lses_s), (c_w_out[o].astype(BF16),), x,
                          "c_out")
            c_p.append(tuple(st_p))
        x = _ffn(x, norm_ff2[layer], ff2_gate[layer].astype(BF16), ff2_up[layer].astype(BF16),
                 ff2_down[layer].astype(BF16), final_w=norm_final if layer == DEPTH - 1 else None)

    def stack(states):
        return [jnp.stack([s[i] for s in states]) for i in range(len(states[0]))]

    a_k_s, a_v_s, conv_s = stack(ab_s)
    c_s = [a.reshape(c.shape) for kv_new, kv in zip(c_new, c_cache) for a, c in zip(kv_new, kv)]
    return (x[:tp].reshape(bp, lp, d), x[tp:].reshape(bs, ls, d),
            *stack(ab_p), *stack(c_p), a_k_s, a_v_s, ssm_s, conv_s, *c_s)
```

```python
import functools
import math

import jax
import jax.numpy as jnp
from jax import lax
from jax.experimental import pallas as pl
from jax.experimental.pallas import tpu as pltpu

F32 = jnp.float32
BF16 = jnp.bfloat16

D_MODEL = 1024
DEPTH = 4
NUM_BUCKETS = 32
MAX_DISTANCE = 2048
Q_BLOCK = 128
RMS_EPS = 1e-6
NEG_INF = -1e30
HQ_A, HKV_A, DH_A, WIN_A = 12, 4, 64, 128
D_INNER = 2 * D_MODEL
P_B = 64
NH_B = D_INNER // P_B
N_B = 128
NG_B = 4
HPG_B = NH_B // NG_B
GW_B = HPG_B * P_B
CONV_W = 4
CONV_DIM = D_INNER + 2 * NG_B * N_B
SSD_CHUNK = 128
C_GROUPS = ((128, 1), (512, 4), (2048, 16))
HPG_C = 4
DH_C = 128
C_OUT = HPG_C * DH_C
AB_Q = HQ_A * DH_A
AB_KV = HKV_A * DH_A
C_IN = 3 * 3 * C_OUT

ABP_Q, ABP_K, ABP_V, ABP_DT = 0, AB_Q, AB_Q + AB_KV, AB_Q + 2 * AB_KV
ABP_Z = ABP_DT + 256
ABP_XS = ABP_Z + D_INNER
ABP_B = ABP_XS + D_INNER
ABP_C = ABP_B + NG_B * N_B
ABP_N = ABP_C + NG_B * N_B

SUBLANES = 8
LANES = 128
VMEM_LIMIT = 56 * 1024 * 1024


def _cparams(n_axes):
    return pltpu.CompilerParams(dimension_semantics=("arbitrary",) * n_axes, vmem_limit_bytes=VMEM_LIMIT)


def _pick(n, options):
    for o in options:
        if n % o == 0:
            return o
    raise ValueError(f"no tile in {options} divides {n}")


def _rms(x, w):
    ms = jnp.mean(x * x, axis=-1, keepdims=True)
    return x * lax.rsqrt(ms + RMS_EPS) * w


def _silu(x):
    return x * jax.nn.sigmoid(x)


def _resident(a):
    return pl.BlockSpec(a.shape, lambda *_: (0,) * a.ndim, pipeline_mode=pl.Buffered(1))


def _ffn_kernel(x_ref, nw_ref, wg_ref, wu_ref, wd_ref, fw_ref, o_ref, *, final_norm):
    x = x_ref[...]
    h = _rms(x, nw_ref[...]).astype(BF16)
    g = jnp.dot(h, wg_ref[...], preferred_element_type=F32)
    u = jnp.dot(h, wu_ref[...], preferred_element_type=F32)
    a = (_silu(g) * u).astype(BF16)
    y = x + 0.5 * jnp.dot(a, wd_ref[...], preferred_element_type=F32)
    if final_norm:
        y = _rms(y, fw_ref[...])
    o_ref[...] = y


def _ffn(x, nw, wg, wu, wd, final_w=None):
    t, d = x.shape
    tm = _pick(t, (512, 256, 128))
    fw = jnp.ones((d,), F32) if final_w is None else final_w
    args = (x, nw.reshape(1, d), wg, wu, wd, fw.reshape(1, d))
    return pl.pallas_call(
        functools.partial(_ffn_kernel, final_norm=final_w is not None),
        out_shape=jax.ShapeDtypeStruct((t, d), F32),
        grid=(t // tm,),
        in_specs=[pl.BlockSpec((tm, d), lambda i: (i, 0))] + [_resident(a) for a in args[1:]],
        out_specs=pl.BlockSpec((tm, d), lambda i: (i, 0)),
        compiler_params=_cparams(1),
        name="ffn",
    )(*args)


def _rms_matmul_kernel(x_ref, nw_ref, w_ref, o_ref):
    h = _rms(x_ref[...], nw_ref[...]).astype(BF16)
    o_ref[...] = jnp.dot(h, w_ref[...], preferred_element_type=F32)


def _rms_matmul(x, nw, w):
    t, d = x.shape
    n = w.shape[1]
    tm = _pick(t, (512, 256, 128))
    args = (x, nw.reshape(1, d), w)
    return pl.pallas_call(
        _rms_matmul_kernel,
        out_shape=jax.ShapeDtypeStruct((t, n), F32),
        grid=(t // tm,),
        in_specs=[pl.BlockSpec((tm, d), lambda i: (i, 0))] + [_resident(a) for a in args[1:]],
        out_specs=pl.BlockSpec((tm, n), lambda i: (i, 0)),
        compiler_params=_cparams(1),
        name="rms_matmul",
    )(*args)


def _t5_bucket(dist):
    n = jnp.maximum(dist, 0)
    max_exact = NUM_BUCKETS // 2
    nf = jnp.maximum(n, 1).astype(F32)
    large = max_exact + (jnp.log(nf / max_exact) / math.log(MAX_DISTANCE / max_exact)
                         * (NUM_BUCKETS - max_exact)).astype(jnp.int32)
    large = jnp.minimum(large, NUM_BUCKETS - 1)
    return jnp.where(n < max_exact, n, large)


def _bias_lookup(bias_heads, dist):
    onehot = jax.nn.one_hot(_t5_bucket(dist), NUM_BUCKETS, dtype=F32)
    return jnp.einsum('...k,kh->...h', onehot, bias_heads.astype(F32), precision=lax.Precision.HIGHEST)


def _band_bias(bias_heads, dil, n_win):
    qi = jnp.arange(Q_BLOCK)[:, None]
    si = jnp.arange(2 * Q_BLOCK)[None, :]
    dist = qi + Q_BLOCK - si
    valid = (dist >= 0) & (dist <= n_win)
    bias = jnp.transpose(_bias_lookup(bias_heads, dist * dil), (2, 0, 1))
    return jnp.where(valid[None], bias, NEG_INF)


def _step_bias(bias_heads, lbuf, ls, dil, n_win, head_rows):
    nh = bias_heads.shape[1]
    i = jnp.arange(ls)[:, None]
    eye = jnp.eye(nh, dtype=bool)

    def table(delta, head_major):
        ok = (delta >= 0) & (delta % dil == 0) & (delta // dil <= n_win)
        b = jnp.transpose(jnp.where(ok[..., None], _bias_lookup(bias_heads, delta), NEG_INF), (2, 0, 1))
        if head_rows and head_major:
            b = jnp.where(eye[:, None, :, None], b[:, :, None, :], NEG_INF)
        elif head_rows:
            b = jnp.where(eye[:, None, None, :], b[:, :, :, None], NEG_INF)
        return b.reshape(nh * ls, -1)

    bias_c = table(lbuf + i - jnp.arange(lbuf)[None, :], False)
    bias_n = table(i - jnp.arange(ls)[None, :], True)
    bias_n = jnp.pad(bias_n, ((0, 0), (0, LANES - bias_n.shape[1])), constant_values=NEG_INF)
    return bias_c, bias_n


def _band_attn_kernel(*refs, hq, hkv, dh, has_sink, emit_lse):
    it = iter(refs)
    q_ref, kp_ref, kc_ref, vp_ref, vc_ref, bias_ref = (next(it) for _ in range(6))
    sink_ref = next(it) if has_sink else None
    o_ref = next(it)
    lse_ref = next(it) if emit_lse else None
    g = hq // hkv
    scale = dh ** -0.5
    col = lax.broadcasted_iota(jnp.int32, (Q_BLOCK, 2 * Q_BLOCK), 1)
    no_prev = col < jnp.where(pl.program_id(2) == 0, Q_BLOCK, 0)
    outs, lses = [], []
    for kv in range(hkv):
        sl = slice(kv * dh, (kv + 1) * dh)
        k = jnp.concatenate([kp_ref[:, sl], kc_ref[:, sl]], axis=0).astype(BF16)
        v = jnp.concatenate([vp_ref[:, sl], vc_ref[:, sl]], axis=0).astype(BF16)
        for gi in range(g):
            h = kv * g + gi
            qh = (q_ref[:, h * dh:(h + 1) * dh] * scale).astype(BF16)
            s = lax.dot_general(qh, k, (((1,), (1,)), ((), ())), preferred_element_type=F32)
            s = jnp.where(no_prev, NEG_INF, s + bias_ref[h])
            m = jnp.max(s, axis=-1, keepdims=True)
            if has_sink:
                m = jnp.maximum(m, sink_ref[h])
            p = jnp.exp(s - m)
            l = jnp.sum(p, axis=-1, keepdims=True)
            if has_sink:
                l = l + jnp.exp(sink_ref[h] - m)
            o = jnp.dot(p.astype(BF16), v, preferred_element_type=F32)
            outs.append(o * (1.0 / l))
            lses.append(m + jnp.log(l))
    o_ref[...] = jnp.concatenate(outs, axis=1).astype(o_ref.dtype)
    if emit_lse:
        lse_ref[...] = _pack_heads(lses)


def _pack_heads(cols):
    n = len(cols)
    rows = cols[0].shape[0]
    lane = lax.broadcasted_iota(jnp.int32, (rows, LANES), 1)
    out = jnp.broadcast_to(cols[-1], (rows, LANES))
    for h in range(n - 2, -1, -1):
        out = jnp.where(lane < (h + 1) * (LANES // n), cols[h], out)
    return out


def _unpack_heads(packed, n):
    rows = packed.shape[0]
    w = LANES // n
    return jnp.concatenate([jnp.broadcast_to(packed[:, h * w:h * w + 1], (rows, LANES)) for h in range(n)], axis=1)


def _band_attn(p, bias, sinks, *, nseq, seqlen, dil, hq, hkv, dh, q_start, k_start, v_start, emit_lse, out_dtype):
    t, ncols = p.shape
    wq, wk = hq * dh, hkv * dh
    assert seqlen % (dil * Q_BLOCK) == 0 and t % dil == 0
    assert q_start % wq == 0 and k_start % wk == 0 and v_start % wk == 0
    assert dil == 1 or (ncols % wq == 0 and ncols % wk == 0)
    nbd = seqlen // dil // Q_BLOCK
    pv = p.reshape(t // dil, dil * ncols)
    tq = nseq * seqlen // dil

    def cur(start, w):
        return lambda b, r, n: (b * nbd + n, r * (ncols // w) + start // w)

    def prev(start, w):
        return lambda b, r, n: (b * nbd + jnp.maximum(n - 1, 0), r * (ncols // w) + start // w)

    in_specs = [
        pl.BlockSpec((Q_BLOCK, wq), cur(q_start, wq)),
        pl.BlockSpec((Q_BLOCK, wk), prev(k_start, wk)),
        pl.BlockSpec((Q_BLOCK, wk), cur(k_start, wk)),
        pl.BlockSpec((Q_BLOCK, wk), prev(v_start, wk)),
        pl.BlockSpec((Q_BLOCK, wk), cur(v_start, wk)),
        pl.BlockSpec((hq, Q_BLOCK, 2 * Q_BLOCK), lambda b, r, n: (0, 0, 0)),
    ]
    args = [pv, pv, pv, pv, pv, bias]
    if sinks is not None:
        in_specs.append(pl.BlockSpec(memory_space=pltpu.SMEM))
        args.append(sinks.astype(F32))
    out_shape = [jax.ShapeDtypeStruct((tq, dil * wq), out_dtype)]
    out_specs = [pl.BlockSpec((Q_BLOCK, wq), lambda b, r, n: (b * nbd + n, r))]
    if emit_lse:
        out_shape.append(jax.ShapeDtypeStruct((tq, dil * LANES), F32))
        out_specs.append(pl.BlockSpec((Q_BLOCK, LANES), lambda b, r, n: (b * nbd + n, r)))
    res = pl.pallas_call(
        functools.partial(_band_attn_kernel, hq=hq, hkv=hkv, dh=dh, has_sink=sinks is not None, emit_lse=emit_lse),
        out_shape=out_shape,
        grid=(nseq, dil, nbd),
        in_specs=in_specs,
        out_specs=out_specs,
        compiler_params=_cparams(3),
        name=f"band_attn_d{dil}",
    )(*args)
    o = res[0].reshape(nseq * seqlen, wq)
    if emit_lse:
        return o, res[1].reshape(nseq * seqlen, LANES)
    return o


def _dil_attn_kernel(q_ref, kp_ref, kc_ref, vp_ref, vc_ref, bias_ref, o_ref, lse_ref, *, dil, dh):
    h = pl.program_id(2)
    col = lax.broadcasted_iota(jnp.int32, (Q_BLOCK, 2 * Q_BLOCK), 1)
    no_prev = col < jnp.where(pl.program_id(1) == 0, Q_BLOCK, 0)
    bias = bias_ref[h]
    scale = dh ** -0.5
    for r in range(dil):
        rows = pl.ds(r, Q_BLOCK, stride=dil)
        qh = (q_ref[rows, :] * scale).astype(BF16)
        k = jnp.concatenate([kp_ref[rows, :], kc_ref[rows, :]], axis=0).astype(BF16)
        v = jnp.concatenate([vp_ref[rows, :], vc_ref[rows, :]], axis=0).astype(BF16)
        s = lax.dot_general(qh, k, (((1,), (1,)), ((), ())), preferred_element_type=F32)
        s = jnp.where(no_prev, NEG_INF, s + bias)
        m = jnp.max(s, axis=-1, keepdims=True)
        p = jnp.exp(s - m)
        l = jnp.sum(p, axis=-1, keepdims=True)
        o = jnp.dot(p.astype(BF16), v, preferred_element_type=F32)
        o_ref[rows, :] = o * (1.0 / l)
        lse_ref[rows, :] = jnp.broadcast_to(m + jnp.log(l), (Q_BLOCK, dh))


def _dil_attn(p, bias, *, nseq, seqlen, dil, nh, dh, q_start, k_start, v_start):
    sb = dil * Q_BLOCK
    assert seqlen % sb == 0 and dh == LANES
    assert q_start % dh == 0 and k_start % dh == 0 and v_start % dh == 0
    nsb = seqlen // sb

    def cur(start):
        return pl.BlockSpec((sb, dh), lambda b, n, h: (b * nsb + n, start // dh + h))

    def prev(start):
        return pl.BlockSpec((sb, dh), lambda b, n, h: (b * nsb + jnp.maximum(n - 1, 0), start // dh + h))

    out_spec = pl.BlockSpec((sb, dh), lambda b, n, h: (b * nsb + n, h))
    return pl.pallas_call(
        functools.partial(_dil_attn_kernel, dil=dil, dh=dh),
        out_shape=[jax.ShapeDtypeStruct((nseq * seqlen, nh * dh), F32)] * 2,
        grid=(nseq, nsb, nh),
        in_specs=[cur(q_start), prev(k_start), cur(k_start), prev(v_start), cur(v_start),
                  pl.BlockSpec((nh, Q_BLOCK, 2 * Q_BLOCK), lambda b, n, h: (0, 0, 0))],
        out_specs=[out_spec, out_spec],
        compiler_params=_cparams(3),
        name=f"dil_attn_d{dil}",
    )(p, p, p, p, p, bias)


def _step_attn_kernel(*refs, ls, hq, hkv, dh, has_sink, emit_lse):
    it = iter(refs)
    q_ref, kn_ref, vn_ref, ck_ref, cv_ref, bc_ref, bn_ref = (next(it) for _ in range(7))
    sink_ref = next(it) if has_sink else None
    o_ref = next(it)
    lse_ref = next(it) if emit_lse else None
    ok_ref, ov_ref = next(it), next(it)
    lbuf = ck_ref.shape[1]
    w = hkv * dh
    g = hq // hkv
    kn, vn = kn_ref[...], vn_ref[...]

    ok_ref[0, :lbuf - ls, :] = ck_ref[0, ls:, :]
    ok_ref[0, lbuf - ls:, :] = kn
    ov_ref[0, :lbuf - ls, :] = cv_ref[0, ls:, :]
    ov_ref[0, lbuf - ls:, :] = vn

    q = q_ref[...] * (dh ** -0.5)
    if g == 1:
        lane = lax.broadcasted_iota(jnp.int32, (ls, w), 1)
        head_of_lane = [(lane >= h * dh) & (lane < (h + 1) * dh) for h in range(hq)]
        qbd = jnp.concatenate([jnp.where(head_of_lane[h], q, 0.0) for h in range(hq)], axis=0)
    else:
        blocks = []
        for h in range(hq):
            kv = h // g
            parts = []
            if kv > 0:
                parts.append(jnp.zeros((ls, kv * dh), F32))
            parts.append(q[:, h * dh:(h + 1) * dh])
            if kv < hkv - 1:
                parts.append(jnp.zeros((ls, (hkv - 1 - kv) * dh), F32))
            blocks.append(jnp.concatenate(parts, axis=1))
        qbd = jnp.concatenate(blocks, axis=0)
    qbd = qbd.astype(BF16)

    pad = jnp.zeros((LANES - ls, w), F32)
    knp = jnp.concatenate([kn, pad], axis=0).astype(BF16)
    vnp = jnp.concatenate([vn, pad], axis=0).astype(BF16)
    nt = (((1,), (1,)), ((), ()))
    sc = lax.dot_general(qbd, ck_ref[0].astype(BF16), nt, preferred_element_type=F32) + bc_ref[...]
    sn = lax.dot_general(qbd, knp, nt, preferred_element_type=F32) + bn_ref[...]
    m = jnp.maximum(jnp.max(sc, axis=-1, keepdims=True), jnp.max(sn, axis=-1, keepdims=True))
    if has_sink:
        m = jnp.maximum(m, sink_ref[...])
    pc = jnp.exp(sc - m)
    pn = jnp.exp(sn - m)
    l = jnp.sum(pc, axis=-1, keepdims=True) + jnp.sum(pn, axis=-1, keepdims=True)
    if has_sink:
        l = l + jnp.exp(sink_ref[...] - m)
    of = (jnp.dot(pc.astype(BF16), cv_ref[0].astype(BF16), preferred_element_type=F32)
          + jnp.dot(pn.astype(BF16), vnp, preferred_element_type=F32)) * (1.0 / l)

    if g == 1:
        o = jnp.where(head_of_lane[0], of[:ls], 0.0)
        for h in range(1, hq):
            o = o + jnp.where(head_of_lane[h], of[h * ls:(h + 1) * ls], 0.0)
    else:
        o = jnp.concatenate([of[h * ls:(h + 1) * ls, (h // g) * dh:(h // g + 1) * dh] for h in range(hq)], axis=1)
    o_ref[...] = o
    if emit_lse:
        lse = m + jnp.log(l)
        lse_ref[...] = _pack_heads([lse[h * ls:(h + 1) * ls] for h in range(hq)])


def _step_attn(p, cache_k, cache_v, bias_c, bias_n, sink_rows, *, row0, ls, hq, hkv, dh, q_start, k_start, v_start,
               emit_lse):
    bs, lbuf, w = cache_k.shape
    wq = hq * dh
    assert ls == SUBLANES and row0 % ls == 0 and w == hkv * dh and lbuf % ls == 0
    assert q_start % wq == 0 and k_start % w == 0 and v_start % w == 0
    r0 = row0 // ls
    in_specs = [
        pl.BlockSpec((ls, wq), lambda b: (r0 + b, q_start // wq)),
        pl.BlockSpec((ls, w), lambda b: (r0 + b, k_start // w)),
        pl.BlockSpec((ls, w), lambda b: (r0 + b, v_start // w)),
        pl.BlockSpec((1, lbuf, w), lambda b: (b, 0, 0)),
        pl.BlockSpec((1, lbuf, w), lambda b: (b, 0, 0)),
        pl.BlockSpec((hq * ls, lbuf), lambda b: (0, 0)),
        pl.BlockSpec((hq * ls, LANES), lambda b: (0, 0)),
    ]
    args = [p, p, p, cache_k, cache_v, bias_c, bias_n]
    if sink_rows is not None:
        in_specs.append(pl.BlockSpec((hq * ls, 1), lambda b: (0, 0)))
        args.append(sink_rows)
    out_shape = [jax.ShapeDtypeStruct((bs * ls, wq), F32)]
    out_specs = [pl.BlockSpec((ls, wq), lambda b: (b, 0))]
    if emit_lse:
        out_shape.append(jax.ShapeDtypeStruct((bs * ls, LANES), F32))
        out_specs.append(pl.BlockSpec((ls, LANES), lambda b: (b, 0)))
    out_shape += [jax.ShapeDtypeStruct((bs, lbuf, w), F32)] * 2
    out_specs += [pl.BlockSpec((1, lbuf, w), lambda b: (b, 0, 0))] * 2
    return pl.pallas_call(
        functools.partial(_step_attn_kernel, ls=ls, hq=hq, hkv=hkv, dh=dh, has_sink=sink_rows is not None,
                          emit_lse=emit_lse),
        out_shape=out_shape,
        grid=(bs,),
        in_specs=in_specs,
        out_specs=out_specs,
        compiler_params=_cparams(1),
        name=f"step_attn_l{lbuf}",
    )(*args)


def _step_attn_rows_kernel(*refs, ls, nh, dh, aliased):
    it = iter(refs)
    q_ref, kn_ref, vn_ref, ck_ref, cv_ref, bc_ref, bn_ref = (next(it) for _ in range(7))
    if aliased:
        next(it), next(it)
    o_ref, lse_ref, ok_ref, ov_ref = (next(it) for _ in range(4))
    rows = ck_ref.shape[0]
    nnew = ls * nh
    kn, vn = kn_ref[...], vn_ref[...]
    heads = [slice(h * dh, (h + 1) * dh) for h in range(nh)]

    ok_ref[:rows - nnew, :] = ck_ref[nnew:, :]
    ov_ref[:rows - nnew, :] = cv_ref[nnew:, :]
    for i in range(ls):
        for h in range(nh):
            r = rows - nnew + i * nh + h
            ok_ref[r:r + 1, :] = kn[i:i + 1, heads[h]]
            ov_ref[r:r + 1, :] = vn[i:i + 1, heads[h]]

    q = q_ref[...] * (dh ** -0.5)
    qr = jnp.concatenate([q[:, hs] for hs in heads], axis=0).astype(BF16)
    pad = [jnp.zeros((LANES - nnew, dh), F32)]
    knr = jnp.concatenate([kn[:, hs] for hs in heads] + pad, axis=0).astype(BF16)
    vnr = jnp.concatenate([vn[:, hs] for hs in heads] + pad, axis=0).astype(BF16)
    nt = (((1,), (1,)), ((), ()))
    sc = lax.dot_general(qr, ck_ref[...].astype(BF16), nt, preferred_element_type=F32) + bc_ref[...]
    sn = lax.dot_general(qr, knr, nt, preferred_element_type=F32) + bn_ref[...]
    m = jnp.maximum(jnp.max(sc, axis=-1, keepdims=True), jnp.max(sn, axis=-1, keepdims=True))
    pc = jnp.exp(sc - m)
    pn = jnp.exp(sn - m)
    l = jnp.sum(pc, axis=-1, keepdims=True) + jnp.sum(pn, axis=-1, keepdims=True)
    of = (jnp.dot(pc.astype(BF16), cv_ref[...].astype(BF16), preferred_element_type=F32)
          + jnp.dot(pn.astype(BF16), vnr, preferred_element_type=F32)) * (1.0 / l)
    o_ref[...] = jnp.concatenate([of[h * ls:(h + 1) * ls] for h in range(nh)], axis=1)
    lse = m + jnp.log(l)
    lse_ref[...] = _pack_heads([lse[h * ls:(h + 1) * ls] for h in range(nh)])


def _step_attn_rows(p, cache_k, cache_v, layer, prev_out, bias_c, bias_n, *, row0, ls, nh, dh, q_start, k_start,
                    v_start):
    nl, bs, rows, _ = cache_k.shape
    w = nh * dh
    assert ls == SUBLANES and row0 % ls == 0 and dh == LANES and rows % (ls * nh) == 0
    assert q_start % w == 0 and k_start % w == 0 and v_start % w == 0
    r0 = row0 // ls
    cache_spec = pl.BlockSpec((None, None, rows, dh), lambda b: (layer, b, 0, 0))
    in_specs = [
        pl.BlockSpec((ls, w), lambda b: (r0 + b, q_start // w)),
        pl.BlockSpec((ls, w), lambda b: (r0 + b, k_start // w)),
        pl.BlockSpec((ls, w), lambda b: (r0 + b, v_start // w)),
        cache_spec, cache_spec,
        pl.BlockSpec((nh * ls, rows), lambda b: (0, 0)),
        pl.BlockSpec((nh * ls, LANES), lambda b: (0, 0)),
    ]
    args = [p, p, p, cache_k, cache_v, bias_c, bias_n]
    aliases = {}
    if prev_out is not None:
        in_specs += [pl.BlockSpec(memory_space=pl.ANY)] * 2
        args += list(prev_out)
        aliases = {7: 2, 8: 3}
    return pl.pallas_call(
        functools.partial(_step_attn_rows_kernel, ls=ls, nh=nh, dh=dh, aliased=prev_out is not None),
        out_shape=[jax.ShapeDtypeStruct((bs * ls, w), F32), jax.ShapeDtypeStruct((bs * ls, LANES), F32),
                   jax.ShapeDtypeStruct(cache_k.shape, F32), jax.ShapeDtypeStruct(cache_v.shape, F32)],
        grid=(bs,),
        in_specs=in_specs,
        out_specs=[pl.BlockSpec((ls, w), lambda b: (b, 0)), pl.BlockSpec((ls, LANES), lambda b: (b, 0)),
                   cache_spec, cache_spec],
        input_output_aliases=aliases,
        compiler_params=_cparams(1),
        name=f"step_attn_rows{rows}",
    )(*args)


def _ssd_kernel(*refs, rows, has_init):
    it = iter(refs)
    z_refs = [next(it) for _ in range(NG_B)]
    xs_refs = [next(it) for _ in range(NG_B)]
    b_ref, c_ref, dt_ref = next(it), next(it), next(it)
    cw_ref, cb_ref, dtb_ref, alog_ref, d_ref, nw_ref = (next(it) for _ in range(6))
    s0_ref, conv0_ref = (next(it), next(it)) if has_init else (None, None)
    y_ref, sout_ref, state, prev, yscr = refs[-5:]
    cl = SSD_CHUNK
    c = pl.program_id(1)

    @pl.when(c == 0)
    def _():
        if has_init:
            state[...] = s0_ref[...]
            prev[...] = conv0_ref[...]
        else:
            state[...] = jnp.zeros_like(state)
            prev[...] = jnp.zeros_like(prev)

    def load(ref):
        u = ref[...]
        if rows < cl:
            u = jnp.concatenate([u, jnp.zeros((cl - rows, u.shape[1]), F32)], axis=0)
        return u

    def conv_silu(u, off):
        w = u.shape[1]
        pv = prev[:, off:off + w]
        cw = cw_ref[:, off:off + w]
        acc = u * cw[CONV_W - 1:CONV_W] + cb_ref[:, off:off + w]
        r8 = lax.broadcasted_iota(jnp.int32, (SUBLANES, w), 0)
        for k in range(1, CONV_W):
            rolled = pltpu.roll(u, k, axis=0)
            top = jnp.where(r8 < k, pltpu.roll(pv, k, axis=0), rolled[:SUBLANES])
            sh = jnp.concatenate([top, rolled[SUBLANES:]], axis=0)
            acc = acc + sh * cw[CONV_W - 1 - k:CONV_W - k]
        if rows == cl:
            prev[:, off:off + w] = u[cl - SUBLANES:]
        return _silu(acc)

    row = lax.broadcasted_iota(jnp.int32, (cl, LANES), 0)
    col = lax.broadcasted_iota(jnp.int32, (cl, LANES), 1)
    x = load(dt_ref) + dtb_ref[...]
    dt = jnp.maximum(x, 0.0) + jnp.log1p(jnp.exp(-jnp.abs(x)))
    if rows < cl:
        dt = jnp.where(row < rows, dt, 0.0)
    acum = dt * (-jnp.exp(alog_ref[...]))
    s = 1
    while s < cl:
        acum = acum + jnp.where(row >= s, pltpu.roll(acum, s, axis=0), 0.0)
        s *= 2
    a_last = acum[cl - 1:cl, :]
    ea = jnp.exp(acum)
    de = jnp.exp(a_last - acum) * dt
    cd = jnp.exp(a_last)
    ac_t, dt_t, de_t = acum.T, dt.T, de.T
    tri = row >= col

    bc = conv_silu(load(b_ref), D_INNER).astype(BF16)
    cc = conv_silu(load(c_ref), D_INNER + NG_B * N_B).astype(BF16)
    nn = (((1,), (0,)), ((), ()))
    nt = (((1,), (1,)), ((), ()))
    for g in range(NG_B):
        bg = bc[:, g * N_B:(g + 1) * N_B]
        cg = cc[:, g * N_B:(g + 1) * N_B]
        cbm = lax.dot_general(cg, bg, nt, preferred_element_type=F32)
        xg = conv_silu(load(xs_refs[g]), g * GW_B)
        xgb = xg.astype(BF16)
        xg_t = xg.T
        for hh in range(HPG_B):
            h = g * HPG_B + hh
            hs = slice(hh * P_B, (hh + 1) * P_B)
            seg = acum[:, h:h + 1] - ac_t[h:h + 1, :]
            dec = jnp.exp(jnp.where(tri, seg, NEG_INF))
            mm = (cbm * dec * dt_t[h:h + 1, :]).astype(BF16)
            sh = state[h]
            yd = lax.dot_general(mm, xgb[:, hs], nn, preferred_element_type=F32)
            yo = lax.dot_general(cg, sh.astype(BF16), nt, preferred_element_type=F32)
            yscr[:, h * P_B:(h + 1) * P_B] = yd + yo * ea[:, h:h + 1] + d_ref[:, h:h + 1] * xg[:, hs]
            xt = (xg_t[hs, :] * de_t[h:h + 1, :]).astype(BF16)
            state[h] = cd[:, h:h + 1] * sh + lax.dot_general(xt, bg, nn, preferred_element_type=F32)
        gs = slice(g * GW_B, (g + 1) * GW_B)
        gt = yscr[:, gs] * _silu(load(z_refs[g]))
        gt = gt * lax.rsqrt(jnp.mean(gt * gt, axis=-1, keepdims=True) + RMS_EPS) * nw_ref[:, gs]
        y_ref[:, gs] = gt[:rows].astype(y_ref.dtype)

    @pl.when(c == pl.num_programs(1) - 1)
    def _():
        sout_ref[...] = state[...]


def _ssd(p, conv_w, conv_b, dt_bias, a_log, d_skip, norm_w, init_state, init_conv, layer=0, prev_out=None, *, row0,
         nseq, seqlen, out_dtype):
    has_init = init_state is not None
    if seqlen % SSD_CHUNK == 0:
        rows, nc = SSD_CHUNK, seqlen // SSD_CHUNK
    else:
        assert seqlen == SUBLANES
        rows, nc = seqlen, 1
    assert row0 % rows == 0
    r0 = row0 // rows

    def blk(w, start):
        return pl.BlockSpec((rows, w), lambda b, c: (r0 + b * nc + c, start // w))

    def whole(a):
        return pl.BlockSpec(a.shape, lambda b, c: (0,) * a.ndim)

    lane_pad = lambda v: jnp.pad(v.astype(F32), (0, LANES - v.shape[0])).reshape(1, LANES)
    params = [conv_w.astype(F32), conv_b.astype(F32).reshape(1, CONV_DIM), lane_pad(dt_bias), lane_pad(a_log),
              lane_pad(d_skip), norm_w.astype(F32).reshape(1, D_INNER)]
    in_specs = ([blk(GW_B, ABP_Z + g * GW_B) for g in range(NG_B)]
                + [blk(GW_B, ABP_XS + g * GW_B) for g in range(NG_B)]
                + [blk(NG_B * N_B, ABP_B), blk(NG_B * N_B, ABP_C), blk(LANES, ABP_DT)]
                + [whole(a) for a in params])
    args = [p] * (2 * NG_B + 3) + params
    aliases = {}
    if has_init:
        state_spec = pl.BlockSpec((None, None, NH_B, P_B, N_B), lambda b, c: (layer, b, 0, 0, 0))
        state_shape = init_state.shape
        in_specs += [state_spec, pl.BlockSpec((None, SUBLANES, CONV_DIM), lambda b, c: (b, 0, 0))]
        args += [init_state, jnp.pad(init_conv.astype(F32), ((0, 0), (SUBLANES - (CONV_W - 1), 0), (0, 0)))]
        if prev_out is not None:
            aliases = {len(args): 1}
            in_specs.append(pl.BlockSpec(memory_space=pl.ANY))
            args.append(prev_out)
    else:
        state_spec = pl.BlockSpec((None, NH_B, P_B, N_B), lambda b, c: (b, 0, 0, 0))
        state_shape = (nseq, NH_B, P_B, N_B)
    return pl.pallas_call(
        functools.partial(_ssd_kernel, rows=rows, has_init=has_init),
        out_shape=[jax.ShapeDtypeStruct((nseq * seqlen, D_INNER), out_dtype),
                   jax.ShapeDtypeStruct(state_shape, F32)],
        grid=(nseq, nc),
        in_specs=in_specs,
        out_specs=[pl.BlockSpec((rows, D_INNER), lambda b, c: (b * nc + c, 0)), state_spec],
        scratch_shapes=[pltpu.VMEM((NH_B, P_B, N_B), F32), pltpu.VMEM((SUBLANES, CONV_DIM), F32),
                        pltpu.VMEM((SSD_CHUNK, D_INNER), F32)],
        input_output_aliases=aliases,
        compiler_params=_cparams(2),
        name=f"ssd_r{rows}",
    )(*args)


def _softplus(x):
    return jnp.maximum(x, 0.0) + jnp.log1p(jnp.exp(-jnp.abs(x)))


def _ssd_step_kernel(*refs, ls, aliased):
    it = iter(refs)
    z_refs = [next(it) for _ in range(NG_B)]
    xs_refs = [next(it) for _ in range(NG_B)]
    b_ref, c_ref, dt_ref = next(it), next(it), next(it)
    cw_ref, cb_ref, dtb_ref, alog_ref, dexp_ref, nw_ref, e_ref, s0_ref, conv0_ref = (next(it) for _ in range(9))
    y_ref, sout_ref = refs[-2:]
    nt = (((1,), (1,)), ((), ()))

    def conv_silu(u, off):
        w = u.shape[1]
        r8 = lax.broadcasted_iota(jnp.int32, (ls, w), 0)
        pv = conv0_ref[:, off:off + w]
        cw = cw_ref[:, off:off + w]
        acc = u * cw[CONV_W - 1:CONV_W] + cb_ref[:, off:off + w]
        for k in range(1, CONV_W):
            sh = jnp.where(r8 < k, pltpu.roll(pv, k, axis=0), pltpu.roll(u, k, axis=0))
            acc = acc + sh * cw[CONV_W - 1 - k:CONV_W - k]
        return _silu(acc)

    row = lax.broadcasted_iota(jnp.int32, (ls, LANES), 0)
    lane = lax.broadcasted_iota(jnp.int32, (ls, LANES), 1)
    dt = _softplus(dt_ref[...] + dtb_ref[...])
    acum = dt * (-jnp.exp(alog_ref[...]))
    s = 1
    while s < ls:
        acum = acum + jnp.where(row >= s, pltpu.roll(acum, s, axis=0), 0.0)
        s *= 2
    a_last = acum[ls - 1:ls, :]
    ea = jnp.exp(acum)
    de = jnp.exp(a_last - acum) * dt
    cd = jnp.exp(a_last)

    bc = conv_silu(b_ref[...], D_INNER)
    cc = conv_silu(c_ref[...], D_INNER + NG_B * N_B)
    xs = jnp.concatenate([conv_silu(xs_refs[g][...], g * GW_B) for g in range(NG_B)], axis=1)
    bgs = [bc[:, g * N_B:(g + 1) * N_B] for g in range(NG_B)]
    cgs = [cc[:, g * N_B:(g + 1) * N_B].astype(BF16) for g in range(NG_B)]
    cbs = [lax.dot_general(cgs[g], bgs[g].astype(BF16), nt, preferred_element_type=F32) for g in range(NG_B)]

    pieces = []
    for s in range(ls):
        cbx = jnp.broadcast_to(cbs[NG_B - 1][:, s:s + 1], (ls, LANES))
        for g in range(NG_B - 2, -1, -1):
            cbx = jnp.where(lane < (g + 1) * HPG_B, cbs[g][:, s:s + 1], cbx)
        dec = jnp.exp(jnp.where(row >= s, acum - acum[s:s + 1, :], NEG_INF))
        pieces.append(cbx * dec * dt[s:s + 1, :])

    def hi_lo(v):
        hi = v.astype(BF16).astype(F32)
        return [hi, v - hi]

    spread = jnp.dot(jnp.concatenate(pieces + hi_lo(ea) + hi_lo(de), axis=0).astype(BF16), e_ref[...],
                     preferred_element_type=F32)
    yd = spread[0:ls] * xs[0:1, :]
    for s in range(1, ls):
        yd = yd + spread[s * ls:(s + 1) * ls] * xs[s:s + 1, :]
    ea_x = spread[ls * ls:ls * ls + ls] + spread[ls * ls + ls:ls * ls + 2 * ls]
    de_x = spread[ls * ls + 2 * ls:ls * ls + 3 * ls] + spread[ls * ls + 3 * ls:ls * ls + 4 * ls]

    yo = jnp.concatenate(
        [lax.dot_general(cgs[g], s0_ref[g * HPG_B:(g + 1) * HPG_B].reshape(GW_B, N_B).astype(BF16), nt,
                         preferred_element_type=F32) for g in range(NG_B)], axis=1)
    y = yd + yo * ea_x + dexp_ref[...] * xs
    xt = xs * de_x
    zero_rows = SSD_CHUNK - ls
    for g in range(NG_B):
        gs = slice(g * GW_B, (g + 1) * GW_B)
        gt = y[:, gs] * _silu(z_refs[g][...])
        y_ref[:, gs] = gt * lax.rsqrt(jnp.mean(gt * gt, axis=-1, keepdims=True) + RMS_EPS) * nw_ref[:, gs]
        xt_t = jnp.concatenate([xt[:, gs], jnp.zeros((zero_rows, GW_B), F32)], axis=0).T.astype(BF16)
        bg = jnp.concatenate([bgs[g], jnp.zeros((zero_rows, N_B), F32)], axis=0).astype(BF16)
        cs = jnp.dot(xt_t, bg, preferred_element_type=F32)
        for hh in range(HPG_B):
            h = g * HPG_B + hh
            sout_ref[h] = cd[:, h:h + 1] * s0_ref[h] + cs[hh * P_B:(hh + 1) * P_B]


def _ssd_step(p, conv_w, conv_b, dt_bias, a_log, d_skip, norm_w, init_state, init_conv, layer, prev_out, *, row0,
              nseq, ls):
    assert ls == SUBLANES and row0 % ls == 0
    r0 = row0 // ls

    def blk(w, start):
        return pl.BlockSpec((ls, w), lambda b: (r0 + b, start // w))

    lane_pad = lambda v: jnp.pad(v.astype(F32), (0, LANES - v.shape[0])).reshape(1, LANES)
    spread = (jnp.arange(LANES)[:, None] == jnp.arange(D_INNER)[None, :] // P_B).astype(BF16)
    params = [conv_w.astype(F32), conv_b.astype(F32).reshape(1, CONV_DIM), lane_pad(dt_bias), lane_pad(a_log),
              jnp.repeat(d_skip.astype(F32), P_B).reshape(1, D_INNER), norm_w.astype(F32).reshape(1, D_INNER), spread]
    state_spec = pl.BlockSpec((None, None, NH_B, P_B, N_B), lambda b: (layer, b, 0, 0, 0))
    in_specs = ([blk(GW_B, ABP_Z + g * GW_B) for g in range(NG_B)]
                + [blk(GW_B, ABP_XS + g * GW_B) for g in range(NG_B)]
                + [blk(NG_B * N_B, ABP_B), blk(NG_B * N_B, ABP_C), blk(LANES, ABP_DT)]
                + [_resident(a) for a in params]
                + [state_spec, pl.BlockSpec((None, SUBLANES, CONV_DIM), lambda b: (b, 0, 0))])
    args = [p] * (2 * NG_B + 3) + params + [
        init_state, jnp.pad(init_conv.astype(F32), ((0, 0), (SUBLANES - (CONV_W - 1), 0), (0, 0)))]
    aliases = {}
    if prev_out is not None:
        aliases = {len(args): 1}
        in_specs.append(pl.BlockSpec(memory_space=pl.ANY))
        args.append(prev_out)
    return pl.pallas_call(
        functools.partial(_ssd_step_kernel, ls=ls, aliased=prev_out is not None),
        out_shape=[jax.ShapeDtypeStruct((nseq * ls, D_INNER), F32), jax.ShapeDtypeStruct(init_state.shape, F32)],
        grid=(nseq,),
        in_specs=in_specs,
        out_specs=[pl.BlockSpec((ls, D_INNER), lambda b: (b, 0)), state_spec],
        input_output_aliases=aliases,
        compiler_params=_cparams(1),
        name="ssd_step",
    )(*args)


def _two_source(np_tiles, body, p_refs, s_refs):
    i = pl.program_id(0)

    @pl.when(i < np_tiles)
    def _():
        body(*[r[...] for r in p_refs])

    @pl.when(i >= np_tiles)
    def _():
        body(*[r[...] for r in s_refs])


def _ab_out_kernel(ap_ref, yp_ref, as_ref, ys_ref, wa_ref, wy_ref, x_ref, o_ref, *, np_tiles):
    def body(att, y):
        o_ref[...] = (x_ref[...] + jnp.dot(att.astype(BF16), wa_ref[...], preferred_element_type=F32)
                      + jnp.dot(y.astype(BF16), wy_ref[...], preferred_element_type=F32))

    _two_source(np_tiles, body, (ap_ref, yp_ref), (as_ref, ys_ref))


def _c_out_kernel(*refs, np_tiles):
    p_refs, s_refs = refs[0:6], refs[6:12]
    w_ref, x_ref, o_ref = refs[12:]

    def body(o1, o2, o3, l1, l2, l3):
        l1, l2, l3 = (l if l.shape[1] == C_OUT else _unpack_heads(l, HPG_C) for l in (l1, l2, l3))
        m = jnp.maximum(jnp.maximum(l1, l2), l3)
        e1, e2, e3 = jnp.exp(l1 - m), jnp.exp(l2 - m), jnp.exp(l3 - m)
        o = (e1 * o1.astype(F32) + e2 * o2.astype(F32) + e3 * o3.astype(F32)) * (1.0 / (e1 + e2 + e3))
        o_ref[...] = x_ref[...] + jnp.dot(o.astype(BF16), w_ref[...], preferred_element_type=F32)

    _two_source(np_tiles, body, p_refs, s_refs)


def _out_proj(kern, prompt_arrs, sample_arrs, weights, x, name):
    t, d = x.shape
    tp, ts = prompt_arrs[0].shape[0], sample_arrs[0].shape[0]
    assert tp + ts == t
    tm = _pick(math.gcd(tp, ts), (512, 256, 128))
    npt = tp // tm
    in_specs = ([pl.BlockSpec((tm, a.shape[1]), lambda i: (jnp.minimum(i, npt - 1), 0)) for a in prompt_arrs]
                + [pl.BlockSpec((tm, a.shape[1]), lambda i: (jnp.maximum(i - npt, 0), 0)) for a in sample_arrs]
                + [pl.BlockSpec(w.shape, lambda i: (0, 0)) for w in weights]
                + [pl.BlockSpec((tm, d), lambda i: (i, 0))])
    return pl.pallas_call(
        functools.partial(kern, np_tiles=npt),
        out_shape=jax.ShapeDtypeStruct((t, d), F32),
        grid=(t // tm,),
        in_specs=in_specs,
        out_specs=pl.BlockSpec((tm, d), lambda i: (i, 0)),
        compiler_params=_cparams(1),
        name=name,
    )(*prompt_arrs, *sample_arrs, *weights, x)


def kernel(x_prompt, x_sample, cache_a_k, cache_a_v, state_b_ssm, state_b_conv, cache_c1_k, cache_c1_v,
           cache_c2_k, cache_c2_v, cache_c3_k, cache_c3_v, rel_bias, norm_ff1, norm_mix, norm_ff2,
           norm_final, ff1_gate, ff1_up, ff1_down, ff2_gate, ff2_up, ff2_down, ab_w_in, ab_w_out, a_sinks,
           b_conv_w, b_conv_b, b_dt_bias, b_a_log, b_d, b_norm_w, c_w_in, c_w_out):
    bp, lp, d = x_prompt.shape
    bs, ls, _ = x_sample.shape
    tp, ts = bp * lp, bs * ls
    c_cache = ((cache_c1_k, cache_c1_v), (cache_c2_k, cache_c2_v), (cache_c3_k, cache_c3_v))
    x = jnp.concatenate([x_prompt.reshape(tp, d), x_sample.reshape(ts, d)], axis=0)

    def tail_rows(p, keep, c0, w):
        return jnp.stack([lax.slice(p, (s * lp + lp - keep, c0), ((s + 1) * lp, c0 + w)) for s in range(bp)])

    a_bias_p = _band_bias(rel_bias[:, :HQ_A], 1, WIN_A)
    ab_p, ab_s, c_p = [], [], []
    ssm_s = None
    c_rows = [[c.reshape(c.shape[0], bs, c.shape[2] * HPG_C, DH_C) for c in kv] for kv in c_cache]
    c_new = [None] * len(C_GROUPS)
    for layer in range(DEPTH):
        x = _ffn(x, norm_ff1[layer], ff1_gate[layer].astype(BF16), ff1_up[layer].astype(BF16),
                 ff1_down[layer].astype(BF16))
        if layer % 2 == 0:
            e = layer // 2
            w = ab_w_in[e]
            xbc0 = AB_Q + 2 * AB_KV + D_INNER
            w_in = jnp.concatenate([w[:, :AB_Q + 2 * AB_KV], w[:, xbc0 + CONV_DIM:],
                                    jnp.zeros((d, ABP_Z - ABP_DT - NH_B), w.dtype),
                                    w[:, AB_Q + 2 * AB_KV:xbc0], w[:, xbc0:xbc0 + CONV_DIM]], axis=1).astype(BF16)
            p = _rms_matmul(x, norm_mix[layer], w_in)
            ssd_params = (b_conv_w[e], b_conv_b[e], b_dt_bias[e], b_a_log[e], b_d[e], b_norm_w[e])

            att_p = _band_attn(p, a_bias_p, a_sinks[e], nseq=bp, seqlen=lp, dil=1, hq=HQ_A, hkv=HKV_A, dh=DH_A,
                               q_start=ABP_Q, k_start=ABP_K, v_start=ABP_V, emit_lse=False, out_dtype=BF16)
            y_p, ssm_p = _ssd(p, *ssd_params, None, None, row0=0, nseq=bp, seqlen=lp, out_dtype=BF16)

            lbuf = cache_a_k.shape[2]
            bias_c, bias_n = _step_bias(rel_bias[:, :HQ_A], lbuf, ls, 1, WIN_A, head_rows=False)
            sink_rows = jnp.repeat(a_sinks[e].astype(F32), ls).reshape(HQ_A * ls, 1)
            att_s, k_s, v_s = _step_attn(p, cache_a_k[e].reshape(bs, lbuf, AB_KV), cache_a_v[e].reshape(bs, lbuf, AB_KV),
                                         bias_c, bias_n, sink_rows, row0=tp, ls=ls, hq=HQ_A, hkv=HKV_A, dh=DH_A,
                                         q_start=ABP_Q, k_start=ABP_K, v_start=ABP_V, emit_lse=False)
            y_s, ssm_s = _ssd_step(p, *ssd_params, state_b_ssm, state_b_conv[e], e, ssm_s, row0=tp, nseq=bs, ls=ls)

            w_out = ab_w_out[e].astype(BF16)
            x = _out_proj(_ab_out_kernel, (att_p, y_p), (att_s, y_s), (w_out[:AB_Q], w_out[AB_Q:]), x, "ab_out")

            keep = min(WIN_A, lp)
            xbc_p = tail_rows(p, min(CONV_W - 1, lp), ABP_XS, CONV_DIM)
            xbc_s = p[tp:, ABP_XS:].reshape(bs, ls, CONV_DIM)
            ab_p.append((tail_rows(p, keep, ABP_K, AB_KV).reshape(bp, keep, HKV_A, DH_A),
                         tail_rows(p, keep, ABP_V, AB_KV).reshape(bp, keep, HKV_A, DH_A),
                         ssm_p,
                         jnp.concatenate([jnp.zeros((bp, CONV_W - 1, CONV_DIM), F32), xbc_p],
                                         axis=1)[:, -(CONV_W - 1):]))
            ab_s.append((k_s.reshape(bs, lbuf, HKV_A, DH_A), v_s.reshape(bs, lbuf, HKV_A, DH_A),
                         jnp.concatenate([state_b_conv[e], xbc_s], axis=1)[:, -(CONV_W - 1):]))
        else:
            o = layer // 2
            p = _rms_matmul(x, norm_mix[layer], c_w_in[o].astype(BF16))
            outs_p, lses_p, outs_s, lses_s, st_p = [], [], [], [], []
            for gi, (win, dil) in enumerate(C_GROUPS):
                n_win = win // dil
                bias_heads = rel_bias[:, gi * HPG_C:(gi + 1) * HPG_C]
                starts = dict(q_start=gi * C_OUT, k_start=(3 + gi) * C_OUT, v_start=(6 + gi) * C_OUT)
                if dil == 1:
                    o_p, l_p = _band_attn(p, _band_bias(bias_heads, dil, n_win), None, nseq=bp, seqlen=lp, dil=dil,
                                          hq=HPG_C, hkv=HPG_C, dh=DH_C, emit_lse=True, out_dtype=BF16, **starts)
                else:
                    o_p, l_p = _dil_attn(p, _band_bias(bias_heads, dil, n_win), nseq=bp, seqlen=lp, dil=dil,
                                         nh=HPG_C, dh=DH_C, **starts)
                lbuf = c_cache[gi][0].shape[2]
                bias_c, bias_n = _step_bias(bias_heads, lbuf, ls, dil, n_win, head_rows=True)
                o_s, l_s, k_s, v_s = _step_attn_rows(p, c_rows[gi][0], c_rows[gi][1], o, c_new[gi], bias_c, bias_n,
                                                     row0=tp, ls=ls, nh=HPG_C, dh=DH_C, **starts)
                c_new[gi] = (k_s, v_s)
                outs_p.append(o_p)
                lses_p.append(l_p)
                outs_s.append(o_s)
                lses_s.append(l_s)
                keep = min(win, lp)
                st_p += [tail_rows(p, keep, starts["k_start"], C_OUT).reshape(bp, keep, HPG_C, DH_C),
                         tail_rows(p, keep, starts["v_start"], C_OUT).reshape(bp, keep, HPG_C, DH_C)]
            x = _out_proj(_c_out_kernel, (*outs_p, *lses_p), (*outs_s, *lses_s), (c_w_out[o].astype(BF16),), x,
                          "c_out")
            c_p.append(tuple(st_p))
        x = _ffn(x, norm_ff2[layer], ff2_gate[layer].astype(BF16), ff2_up[layer].astype(BF16),
                 ff2_down[layer].astype(BF16), final_w=norm_final if layer == DEPTH - 1 else None)

    def stack(states):
        return [jnp.stack([s[i] for s in states]) for i in range(len(states[0]))]

    a_k_s, a_v_s, conv_s = stack(ab_s)
    c_s = [a.reshape(c.shape) for kv_new, kv in zip(c_new, c_cache) for a, c in zip(kv_new, kv)]
    return (x[:tp].reshape(bp, lp, d), x[tp:].reshape(bs, ls, d),
            *stack(ab_p), *stack(c_p), a_k_s, a_v_s, ssm_s, conv_s, *c_s)
```

```python
import functools
import math

import jax
import jax.numpy as jnp
from jax import lax
from jax.experimental import pallas as pl
from jax.experimental.pallas import tpu as pltpu

F32 = jnp.float32
BF16 = jnp.bfloat16

D_MODEL = 1024
DEPTH = 4
NUM_BUCKETS = 32
MAX_DISTANCE = 2048
Q_BLOCK = 128
RMS_EPS = 1e-6
NEG_INF = -1e30
HQ_A, HKV_A, DH_A, WIN_A = 12, 4, 64, 128
D_INNER = 2 * D_MODEL
P_B = 64
NH_B = D_INNER // P_B
N_B = 128
NG_B = 4
HPG_B = NH_B // NG_B
GW_B = HPG_B * P_B
CONV_W = 4
CONV_DIM = D_INNER + 2 * NG_B * N_B
SSD_CHUNK = 128
C_GROUPS = ((128, 1), (512, 4), (2048, 16))
HPG_C = 4
DH_C = 128
C_OUT = HPG_C * DH_C
AB_Q = HQ_A * DH_A
AB_KV = HKV_A * DH_A
C_IN = 3 * 3 * C_OUT

ABP_Q, ABP_K, ABP_V, ABP_DT = 0, AB_Q, AB_Q + AB_KV, AB_Q + 2 * AB_KV
ABP_Z = ABP_DT + 256
ABP_XS = ABP_Z + D_INNER
ABP_B = ABP_XS + D_INNER
ABP_C = ABP_B + NG_B * N_B
ABP_N = ABP_C + NG_B * N_B

SUBLANES = 8
LANES = 128
VMEM_LIMIT = 56 * 1024 * 1024


def _cparams(n_axes):
    return pltpu.CompilerParams(dimension_semantics=("arbitrary",) * n_axes, vmem_limit_bytes=VMEM_LIMIT)


def _pick(n, options):
    for o in options:
        if n % o == 0:
            return o
    raise ValueError(f"no tile in {options} divides {n}")


def _rms(x, w):
    ms = jnp.mean(x * x, axis=-1, keepdims=True)
    return x * lax.rsqrt(ms + RMS_EPS) * w


def _silu(x):
    return x * jax.nn.sigmoid(x)


def _resident(a):
    return pl.BlockSpec(a.shape, lambda *_: (0,) * a.ndim, pipeline_mode=pl.Buffered(1))


def _ffn_kernel(x_ref, nw_ref, wg_ref, wu_ref, wd_ref, fw_ref, o_ref, *, final_norm):
    x = x_ref[...]
    h = _rms(x, nw_ref[...]).astype(BF16)
    g = jnp.dot(h, wg_ref[...], preferred_element_type=F32)
    u = jnp.dot(h, wu_ref[...], preferred_element_type=F32)
    a = (_silu(g) * u).astype(BF16)
    y = x + 0.5 * jnp.dot(a, wd_ref[...], preferred_element_type=F32)
    if final_norm:
        y = _rms(y, fw_ref[...])
    o_ref[...] = y


def _ffn(x, nw, wg, wu, wd, final_w=None):
    t, d = x.shape
    tm = _pick(t, (512, 256, 128))
    fw = jnp.ones((d,), F32) if final_w is None else final_w
    args = (x, nw.reshape(1, d), wg, wu, wd, fw.reshape(1, d))
    return pl.pallas_call(
        functools.partial(_ffn_kernel, final_norm=final_w is not None),
        out_shape=jax.ShapeDtypeStruct((t, d), F32),
        grid=(t // tm,),
        in_specs=[pl.BlockSpec((tm, d), lambda i: (i, 0))] + [_resident(a) for a in args[1:]],
        out_specs=pl.BlockSpec((tm, d), lambda i: (i, 0)),
        compiler_params=_cparams(1),
        name="ffn",
    )(*args)


def _rms_matmul_kernel(x_ref, nw_ref, w_ref, o_ref):
    h = _rms(x_ref[...], nw_ref[...]).astype(BF16)
    o_ref[...] = jnp.dot(h, w_ref[...], preferred_element_type=F32)


def _rms_matmul(x, nw, w):
    t, d = x.shape
    n = w.shape[1]
    tm = _pick(t, (512, 256, 128))
    args = (x, nw.reshape(1, d), w)
    return pl.pallas_call(
        _rms_matmul_kernel,
        out_shape=jax.ShapeDtypeStruct((t, n), F32),
        grid=(t // tm,),
        in_specs=[pl.BlockSpec((tm, d), lambda i: (i, 0))] + [_resident(a) for a in args[1:]],
        out_specs=pl.BlockSpec((tm, n), lambda i: (i, 0)),
        compiler_params=_cparams(1),
        name="rms_matmul",
    )(*args)


def _t5_bucket(dist):
    n = jnp.maximum(dist, 0)
    max_exact = NUM_BUCKETS // 2
    nf = jnp.maximum(n, 1).astype(F32)
    large = max_exact + (jnp.log(nf / max_exact) / math.log(MAX_DISTANCE / max_exact)
                         * (NUM_BUCKETS - max_exact)).astype(jnp.int32)
    large = jnp.minimum(large, NUM_BUCKETS - 1)
    return jnp.where(n < max_exact, n, large)


def _bias_lookup(bias_heads, dist):
    onehot = jax.nn.one_hot(_t5_bucket(dist), NUM_BUCKETS, dtype=F32)
    return jnp.einsum('...k,kh->...h', onehot, bias_heads.astype(F32), precision=lax.Precision.HIGHEST)


def _band_bias(bias_heads, dil, n_win):
    qi = jnp.arange(Q_BLOCK)[:, None]
    si = jnp.arange(2 * Q_BLOCK)[None, :]
    dist = qi + Q_BLOCK - si
    valid = (dist >= 0) & (dist <= n_win)
    bias = jnp.transpose(_bias_lookup(bias_heads, dist * dil), (2, 0, 1))
    return jnp.where(valid[None], bias, NEG_INF)


def _step_bias(bias_heads, lbuf, ls, dil, n_win, head_rows):
    nh = bias_heads.shape[1]
    i = jnp.arange(ls)[:, None]
    eye = jnp.eye(nh, dtype=bool)

    def table(delta, head_major):
        ok = (delta >= 0) & (delta % dil == 0) & (delta // dil <= n_win)
        b = jnp.transpose(jnp.where(ok[..., None], _bias_lookup(bias_heads, delta), NEG_INF), (2, 0, 1))
        if head_rows and head_major:
            b = jnp.where(eye[:, None, :, None], b[:, :, None, :], NEG_INF)
        elif head_rows:
            b = jnp.where(eye[:, None, None, :], b[:, :, :, None], NEG_INF)
        return b.reshape(nh * ls, -1)

    bias_c = table(lbuf + i - jnp.arange(lbuf)[None, :], False)
    bias_n = table(i - jnp.arange(ls)[None, :], True)
    bias_n = jnp.pad(bias_n, ((0, 0), (0, LANES - bias_n.shape[1])), constant_values=NEG_INF)
    return bias_c, bias_n


def _band_attn_kernel(*refs, hq, hkv, dh, has_sink, emit_lse):
    it = iter(refs)
    q_ref, kp_ref, kc_ref, vp_ref, vc_ref, bias_ref = (next(it) for _ in range(6))
    sink_ref = next(it) if has_sink else None
    o_ref = next(it)
    lse_ref = next(it) if emit_lse else None
    g = hq // hkv
    scale = dh ** -0.5
    col = lax.broadcasted_iota(jnp.int32, (Q_BLOCK, 2 * Q_BLOCK), 1)
    no_prev = col < jnp.where(pl.program_id(1) == 0, Q_BLOCK, 0)
    outs, lses = [], []
    for kv in range(hkv):
        sl = slice(kv * dh, (kv + 1) * dh)
        k = jnp.concatenate([kp_ref[:, sl], kc_ref[:, sl]], axis=0).astype(BF16)
        v = jnp.concatenate([vp_ref[:, sl], vc_ref[:, sl]], axis=0).astype(BF16)
        for gi in range(g):
            h = kv * g + gi
            qh = (q_ref[:, h * dh:(h + 1) * dh] * scale).astype(BF16)
            s = lax.dot_general(qh, k, (((1,), (1,)), ((), ())), preferred_element_type=F32)
            s = jnp.where(no_prev, NEG_INF, s + bias_ref[h])
            m = jnp.max(s, axis=-1, keepdims=True)
            if has_sink:
                m = jnp.maximum(m, sink_ref[h])
            p = jnp.exp(s - m)
            l = jnp.sum(p, axis=-1, keepdims=True)
            if has_sink:
                l = l + jnp.exp(sink_ref[h] - m)
            o = jnp.dot(p.astype(BF16), v, preferred_element_type=F32)
            outs.append(o * (1.0 / l))
            lses.append(m + jnp.log(l))
    o_ref[...] = jnp.concatenate(outs, axis=1).astype(o_ref.dtype)
    if emit_lse:
        lse_ref[...] = _pack_heads(lses)


def _pack_heads(cols):
    n = len(cols)
    rows = cols[0].shape[0]
    lane = lax.broadcasted_iota(jnp.int32, (rows, LANES), 1)
    out = jnp.broadcast_to(cols[-1], (rows, LANES))
    for h in range(n - 2, -1, -1):
        out = jnp.where(lane < (h + 1) * (LANES // n), cols[h], out)
    return out


def _unpack_heads(packed, n):
    rows = packed.shape[0]
    w = LANES // n
    return jnp.concatenate([jnp.broadcast_to(packed[:, h * w:h * w + 1], (rows, LANES)) for h in range(n)], axis=1)


def _band_attn(p, bias, sinks, *, nseq, seqlen, hq, hkv, dh, q_start, k_start, v_start, emit_lse, out_dtype):
    wq, wk = hq * dh, hkv * dh
    assert seqlen % Q_BLOCK == 0
    assert q_start % wq == 0 and k_start % wk == 0 and v_start % wk == 0
    nb = seqlen // Q_BLOCK

    def cur(start, w):
        return lambda b, n: (b * nb + n, start // w)

    def prev(start, w):
        return lambda b, n: (b * nb + jnp.maximum(n - 1, 0), start // w)

    in_specs = [
        pl.BlockSpec((Q_BLOCK, wq), cur(q_start, wq)),
        pl.BlockSpec((Q_BLOCK, wk), prev(k_start, wk)),
        pl.BlockSpec((Q_BLOCK, wk), cur(k_start, wk)),
        pl.BlockSpec((Q_BLOCK, wk), prev(v_start, wk)),
        pl.BlockSpec((Q_BLOCK, wk), cur(v_start, wk)),
        pl.BlockSpec((hq, Q_BLOCK, 2 * Q_BLOCK), lambda b, n: (0, 0, 0)),
    ]
    args = [p, p, p, p, p, bias]
    if sinks is not None:
        in_specs.append(pl.BlockSpec(memory_space=pltpu.SMEM))
        args.append(sinks.astype(F32))
    out_shape = [jax.ShapeDtypeStruct((nseq * seqlen, wq), out_dtype)]
    out_specs = [pl.BlockSpec((Q_BLOCK, wq), lambda b, n: (b * nb + n, 0))]
    if emit_lse:
        out_shape.append(jax.ShapeDtypeStruct((nseq * seqlen, LANES), F32))
        out_specs.append(pl.BlockSpec((Q_BLOCK, LANES), lambda b, n: (b * nb + n, 0)))
    res = pl.pallas_call(
        functools.partial(_band_attn_kernel, hq=hq, hkv=hkv, dh=dh, has_sink=sinks is not None, emit_lse=emit_lse),
        out_shape=out_shape,
        grid=(nseq, nb),
        in_specs=in_specs,
        out_specs=out_specs,
        compiler_params=_cparams(2),
        name="band_attn",
    )(*args)
    return tuple(res) if emit_lse else res[0]


def _dil_attn_kernel(q_ref, kp_ref, kc_ref, vp_ref, vc_ref, bias_ref, o_ref, lse_ref, *, dil, dh):
    h = pl.program_id(2)
    col = lax.broadcasted_iota(jnp.int32, (Q_BLOCK, 2 * Q_BLOCK), 1)
    no_prev = col < jnp.where(pl.program_id(1) == 0, Q_BLOCK, 0)
    bias = bias_ref[h]
    scale = dh ** -0.5
    for r in range(dil):
        rows = pl.ds(r, Q_BLOCK, stride=dil)
        qh = (q_ref[rows, :] * scale).astype(BF16)
        k = jnp.concatenate([kp_ref[rows, :], kc_ref[rows, :]], axis=0).astype(BF16)
        v = jnp.concatenate([vp_ref[rows, :], vc_ref[rows, :]], axis=0).astype(BF16)
        s = lax.dot_general(qh, k, (((1,), (1,)), ((), ())), preferred_element_type=F32)
        s = jnp.where(no_prev, NEG_INF, s + bias)
        m = jnp.max(s, axis=-1, keepdims=True)
        p = jnp.exp(s - m)
        l = jnp.sum(p, axis=-1, keepdims=True)
        o = jnp.dot(p.astype(BF16), v, preferred_element_type=F32)
        o_ref[rows, :] = o * (1.0 / l)
        lse_ref[rows, :] = jnp.broadcast_to(m + jnp.log(l), (Q_BLOCK, dh))


def _dil_attn(p, bias, *, nseq, seqlen, dil, nh, dh, q_start, k_start, v_start):
    sb = dil * Q_BLOCK
    assert seqlen % sb == 0 and dh == LANES
    assert q_start % dh == 0 and k_start % dh == 0 and v_start % dh == 0
    nsb = seqlen // sb

    def cur(start):
        return pl.BlockSpec((sb, dh), lambda b, n, h: (b * nsb + n, start // dh + h))

    def prev(start):
        return pl.BlockSpec((sb, dh), lambda b, n, h: (b * nsb + jnp.maximum(n - 1, 0), start // dh + h))

    out_spec = pl.BlockSpec((sb, dh), lambda b, n, h: (b * nsb + n, h))
    return pl.pallas_call(
        functools.partial(_dil_attn_kernel, dil=dil, dh=dh),
        out_shape=[jax.ShapeDtypeStruct((nseq * seqlen, nh * dh), F32)] * 2,
        grid=(nseq, nsb, nh),
        in_specs=[cur(q_start), prev(k_start), cur(k_start), prev(v_start), cur(v_start),
                  pl.BlockSpec((nh, Q_BLOCK, 2 * Q_BLOCK), lambda b, n, h: (0, 0, 0))],
        out_specs=[out_spec, out_spec],
        compiler_params=_cparams(3),
        name=f"dil_attn_d{dil}",
    )(p, p, p, p, p, bias)


def _step_attn_kernel(*refs, ls, hq, hkv, dh, has_sink, emit_lse):
    it = iter(refs)
    q_ref, kn_ref, vn_ref, ck_ref, cv_ref, bc_ref, bn_ref = (next(it) for _ in range(7))
    sink_ref = next(it) if has_sink else None
    o_ref = next(it)
    lse_ref = next(it) if emit_lse else None
    ok_ref, ov_ref = next(it), next(it)
    lbuf = ck_ref.shape[1]
    w = hkv * dh
    g = hq // hkv
    kn, vn = kn_ref[...], vn_ref[...]

    ok_ref[0, :lbuf - ls, :] = ck_ref[0, ls:, :]
    ok_ref[0, lbuf - ls:, :] = kn
    ov_ref[0, :lbuf - ls, :] = cv_ref[0, ls:, :]
    ov_ref[0, lbuf - ls:, :] = vn

    q = q_ref[...] * (dh ** -0.5)
    if g == 1:
        lane = lax.broadcasted_iota(jnp.int32, (ls, w), 1)
        head_of_lane = [(lane >= h * dh) & (lane < (h + 1) * dh) for h in range(hq)]
        qbd = jnp.concatenate([jnp.where(head_of_lane[h], q, 0.0) for h in range(hq)], axis=0)
    else:
        blocks = []
        for h in range(hq):
            kv = h // g
            parts = []
            if kv > 0:
                parts.append(jnp.zeros((ls, kv * dh), F32))
            parts.append(q[:, h * dh:(h + 1) * dh])
            if kv < hkv - 1:
                parts.append(jnp.zeros((ls, (hkv - 1 - kv) * dh), F32))
            blocks.append(jnp.concatenate(parts, axis=1))
        qbd = jnp.concatenate(blocks, axis=0)
    qbd = qbd.astype(BF16)

    pad = jnp.zeros((LANES - ls, w), F32)
    knp = jnp.concatenate([kn, pad], axis=0).astype(BF16)
    vnp = jnp.concatenate([vn, pad], axis=0).astype(BF16)
    nt = (((1,), (1,)), ((), ()))
    sc = lax.dot_general(qbd, ck_ref[0].astype(BF16), nt, preferred_element_type=F32) + bc_ref[...]
    sn = lax.dot_general(qbd, knp, nt, preferred_element_type=F32) + bn_ref[...]
    m = jnp.maximum(jnp.max(sc, axis=-1, keepdims=True), jnp.max(sn, axis=-1, keepdims=True))
    if has_sink:
        m = jnp.maximum(m, sink_ref[...])
    pc = jnp.exp(sc - m)
    pn = jnp.exp(sn - m)
    l = jnp.sum(pc, axis=-1, keepdims=True) + jnp.sum(pn, axis=-1, keepdims=True)
    if has_sink:
        l = l + jnp.exp(sink_ref[...] - m)
    of = (jnp.dot(pc.astype(BF16), cv_ref[0].astype(BF16), preferred_element_type=F32)
          + jnp.dot(pn.astype(BF16), vnp, preferred_element_type=F32)) * (1.0 / l)

    if g == 1:
        o = jnp.where(head_of_lane[0], of[:ls], 0.0)
        for h in range(1, hq):
            o = o + jnp.where(head_of_lane[h], of[h * ls:(h + 1) * ls], 0.0)
    else:
        o = jnp.concatenate([of[h * ls:(h + 1) * ls, (h // g) * dh:(h // g + 1) * dh] for h in range(hq)], axis=1)
    o_ref[...] = o
    if emit_lse:
        lse = m + jnp.log(l)
        lse_ref[...] = _pack_heads([lse[h * ls:(h + 1) * ls] for h in range(hq)])


def _step_attn(p, cache_k, cache_v, bias_c, bias_n, sink_rows, *, row0, ls, hq, hkv, dh, q_start, k_start, v_start,
               emit_lse):
    bs, lbuf, w = cache_k.shape
    wq = hq * dh
    assert ls == SUBLANES and row0 % ls == 0 and w == hkv * dh and lbuf % ls == 0
    assert q_start % wq == 0 and k_start % w == 0 and v_start % w == 0
    r0 = row0 // ls
    in_specs = [
        pl.BlockSpec((ls, wq), lambda b: (r0 + b, q_start // wq)),
        pl.BlockSpec((ls, w), lambda b: (r0 + b, k_start // w)),
        pl.BlockSpec((ls, w), lambda b: (r0 + b, v_start // w)),
        pl.BlockSpec((1, lbuf, w), lambda b: (b, 0, 0)),
        pl.BlockSpec((1, lbuf, w), lambda b: (b, 0, 0)),
        pl.BlockSpec((hq * ls, lbuf), lambda b: (0, 0)),
        pl.BlockSpec((hq * ls, LANES), lambda b: (0, 0)),
    ]
    args = [p, p, p, cache_k, cache_v, bias_c, bias_n]
    if sink_rows is not None:
        in_specs.append(pl.BlockSpec((hq * ls, 1), lambda b: (0, 0)))
        args.append(sink_rows)
    out_shape = [jax.ShapeDtypeStruct((bs * ls, wq), F32)]
    out_specs = [pl.BlockSpec((ls, wq), lambda b: (b, 0))]
    if emit_lse:
        out_shape.append(jax.ShapeDtypeStruct((bs * ls, LANES), F32))
        out_specs.append(pl.BlockSpec((ls, LANES), lambda b: (b, 0)))
    out_shape += [jax.ShapeDtypeStruct((bs, lbuf, w), F32)] * 2
    out_specs += [pl.BlockSpec((1, lbuf, w), lambda b: (b, 0, 0))] * 2
    return pl.pallas_call(
        functools.partial(_step_attn_kernel, ls=ls, hq=hq, hkv=hkv, dh=dh, has_sink=sink_rows is not None,
                          emit_lse=emit_lse),
        out_shape=out_shape,
        grid=(bs,),
        in_specs=in_specs,
        out_specs=out_specs,
        compiler_params=_cparams(1),
        name=f"step_attn_l{lbuf}",
    )(*args)


def _step_attn_rows_kernel(*refs, ls, nh, dh, aliased):
    it = iter(refs)
    q_ref, kn_ref, vn_ref, ck_ref, cv_ref, bc_ref, bn_ref = (next(it) for _ in range(7))
    if aliased:
        next(it), next(it)
    o_ref, lse_ref, ok_ref, ov_ref = (next(it) for _ in range(4))
    rows = ck_ref.shape[0]
    nnew = ls * nh
    kn, vn = kn_ref[...], vn_ref[...]
    heads = [slice(h * dh, (h + 1) * dh) for h in range(nh)]

    ok_ref[:rows - nnew, :] = ck_ref[nnew:, :]
    ov_ref[:rows - nnew, :] = cv_ref[nnew:, :]
    for i in range(ls):
        for h in range(nh):
            r = rows - nnew + i * nh + h
            ok_ref[r:r + 1, :] = kn[i:i + 1, heads[h]]
            ov_ref[r:r + 1, :] = vn[i:i + 1, heads[h]]

    q = q_ref[...] * (dh ** -0.5)
    qr = jnp.concatenate([q[:, hs] for hs in heads], axis=0).astype(BF16)
    pad = [jnp.zeros((LANES - nnew, dh), F32)]
    knr = jnp.concatenate([kn[:, hs] for hs in heads] + pad, axis=0).astype(BF16)
    vnr = jnp.concatenate([vn[:, hs] for hs in heads] + pad, axis=0).astype(BF16)
    nt = (((1,), (1,)), ((), ()))
    sc = lax.dot_general(qr, ck_ref[...].astype(BF16), nt, preferred_element_type=F32) + bc_ref[...]
    sn = lax.dot_general(qr, knr, nt, preferred_element_type=F32) + bn_ref[...]
    m = jnp.maximum(jnp.max(sc, axis=-1, keepdims=True), jnp.max(sn, axis=-1, keepdims=True))
    pc = jnp.exp(sc - m)
    pn = jnp.exp(sn - m)
    l = jnp.sum(pc, axis=-1, keepdims=True) + jnp.sum(pn, axis=-1, keepdims=True)
    of = (jnp.dot(pc.astype(BF16), cv_ref[...].astype(BF16), preferred_element_type=F32)
          + jnp.dot(pn.astype(BF16), vnr, preferred_element_type=F32)) * (1.0 / l)
    o_ref[...] = jnp.concatenate([of[h * ls:(h + 1) * ls] for h in range(nh)], axis=1)
    lse = m + jnp.log(l)
    lse_ref[...] = _pack_heads([lse[h * ls:(h + 1) * ls] for h in range(nh)])


def _step_attn_rows(p, cache_k, cache_v, layer, prev_out, bias_c, bias_n, *, row0, ls, nh, dh, q_start, k_start,
                    v_start):
    nl, bs, rows, _ = cache_k.shape
    w = nh * dh
    assert ls == SUBLANES and row0 % ls == 0 and dh == LANES and rows % (ls * nh) == 0
    assert q_start % w == 0 and k_start % w == 0 and v_start % w == 0
    r0 = row0 // ls
    cache_spec = pl.BlockSpec((None, None, rows, dh), lambda b: (layer, b, 0, 0))
    in_specs = [
        pl.BlockSpec((ls, w), lambda b: (r0 + b, q_start // w)),
        pl.BlockSpec((ls, w), lambda b: (r0 + b, k_start // w)),
        pl.BlockSpec((ls, w), lambda b: (r0 + b, v_start // w)),
        cache_spec, cache_spec,
        pl.BlockSpec((nh * ls, rows), lambda b: (0, 0)),
        pl.BlockSpec((nh * ls, LANES), lambda b: (0, 0)),
    ]
    args = [p, p, p, cache_k, cache_v, bias_c, bias_n]
    aliases = {}
    if prev_out is not None:
        in_specs += [pl.BlockSpec(memory_space=pl.ANY)] * 2
        args += list(prev_out)
        aliases = {7: 2, 8: 3}
    return pl.pallas_call(
        functools.partial(_step_attn_rows_kernel, ls=ls, nh=nh, dh=dh, aliased=prev_out is not None),
        out_shape=[jax.ShapeDtypeStruct((bs * ls, w), F32), jax.ShapeDtypeStruct((bs * ls, LANES), F32),
                   jax.ShapeDtypeStruct(cache_k.shape, F32), jax.ShapeDtypeStruct(cache_v.shape, F32)],
        grid=(bs,),
        in_specs=in_specs,
        out_specs=[pl.BlockSpec((ls, w), lambda b: (b, 0)), pl.BlockSpec((ls, LANES), lambda b: (b, 0)),
                   cache_spec, cache_spec],
        input_output_aliases=aliases,
        compiler_params=_cparams(1),
        name=f"step_attn_rows{rows}",
    )(*args)


def _ssd_kernel(*refs):
    it = iter(refs)
    z_refs = [next(it) for _ in range(NG_B)]
    xs_refs = [next(it) for _ in range(NG_B)]
    b_ref, c_ref, dt_ref = next(it), next(it), next(it)
    cw_ref, cb_ref, dtb_ref, alog_ref, dexp_ref, nw_ref = (next(it) for _ in range(6))
    y_ref, sout_ref, state, prev, yscr = refs[-5:]
    cl = SSD_CHUNK
    c = pl.program_id(1)

    @pl.when(c == 0)
    def _():
        state[...] = jnp.zeros_like(state)
        prev[...] = jnp.zeros_like(prev)

    def conv_silu(u, off):
        w = u.shape[1]
        pv = prev[:, off:off + w]
        cw = cw_ref[:, off:off + w]
        acc = u * cw[CONV_W - 1:CONV_W] + cb_ref[:, off:off + w]
        r8 = lax.broadcasted_iota(jnp.int32, (SUBLANES, w), 0)
        for k in range(1, CONV_W):
            rolled = pltpu.roll(u, k, axis=0)
            top = jnp.where(r8 < k, pltpu.roll(pv, k, axis=0), rolled[:SUBLANES])
            sh = jnp.concatenate([top, rolled[SUBLANES:]], axis=0)
            acc = acc + sh * cw[CONV_W - 1 - k:CONV_W - k]
        prev[:, off:off + w] = u[cl - SUBLANES:]
        return _silu(acc)

    row = lax.broadcasted_iota(jnp.int32, (cl, LANES), 0)
    col = lax.broadcasted_iota(jnp.int32, (cl, LANES), 1)
    dt = _softplus(dt_ref[...] + dtb_ref[...])
    acum = dt * (-jnp.exp(alog_ref[...]))
    s = 1
    while s < cl:
        acum = acum + jnp.where(row >= s, pltpu.roll(acum, s, axis=0), 0.0)
        s *= 2
    a_last = acum[cl - 1:cl, :]
    ea = jnp.exp(acum)
    de = jnp.exp(a_last - acum) * dt
    cd = jnp.exp(a_last)
    ac_t, dt_t, de_t = acum.T, dt.T, de.T
    tri = row >= col
    first = col < P_B
    first_rows = row < P_B
    pair = 2 * P_B

    bc = conv_silu(b_ref[...], D_INNER).astype(BF16)
    cc = conv_silu(c_ref[...], D_INNER + NG_B * N_B).astype(BF16)
    nn = (((1,), (0,)), ((), ()))
    nt = (((1,), (1,)), ((), ()))
    for g in range(NG_B):
        bg = bc[:, g * N_B:(g + 1) * N_B]
        cg = cc[:, g * N_B:(g + 1) * N_B]
        cbm = lax.dot_general(cg, bg, nt, preferred_element_type=F32)
        xg = conv_silu(xs_refs[g][...], g * GW_B)
        xg_t = xg.T
        for pr in range(HPG_B // 2):
            h1 = g * HPG_B + 2 * pr
            h2 = h1 + 1
            ps = slice(pr * pair, (pr + 1) * pair)
            cs = slice(g * GW_B + pr * pair, g * GW_B + (pr + 1) * pair)
            mms = []
            for h in (h1, h2):
                dec = jnp.exp(jnp.where(tri, acum[:, h:h + 1] - ac_t[h:h + 1, :], NEG_INF))
                mms.append((cbm * dec * dt_t[h:h + 1, :]).astype(BF16))
            xp = xg[:, ps]
            xbd = jnp.concatenate([jnp.where(first, xp, 0.0).astype(BF16), jnp.where(first, 0.0, xp).astype(BF16)],
                                  axis=0)
            yd = lax.dot_general(jnp.concatenate(mms, axis=1), xbd, nn, preferred_element_type=F32)
            sp = state[h1 * P_B:(h2 + 1) * P_B, :]
            yo = lax.dot_general(cg, sp.astype(BF16), nt, preferred_element_type=F32)
            ea2 = jnp.where(first, ea[:, h1:h1 + 1], ea[:, h2:h2 + 1])
            yscr[:, cs] = yd + yo * ea2 + dexp_ref[:, cs] * xp
            de2 = jnp.where(first_rows, de_t[h1:h1 + 1, :], de_t[h2:h2 + 1, :])
            cd2 = jnp.where(first_rows, cd[:, h1:h1 + 1], cd[:, h2:h2 + 1])
            xt = (xg_t[ps, :] * de2).astype(BF16)
            state[h1 * P_B:(h2 + 1) * P_B, :] = cd2 * sp + lax.dot_general(xt, bg, nn, preferred_element_type=F32)
        gs = slice(g * GW_B, (g + 1) * GW_B)
        gt = yscr[:, gs] * _silu(z_refs[g][...])
        gt = gt * lax.rsqrt(jnp.mean(gt * gt, axis=-1, keepdims=True) + RMS_EPS) * nw_ref[:, gs]
        y_ref[:, gs] = gt.astype(y_ref.dtype)

    @pl.when(c == pl.num_programs(1) - 1)
    def _():
        sout_ref[...] = state[...].reshape(NH_B, P_B, N_B)


def _ssd(p, conv_w, conv_b, dt_bias, a_log, d_skip, norm_w, *, nseq, seqlen, out_dtype):
    assert seqlen % SSD_CHUNK == 0
    nc = seqlen // SSD_CHUNK

    def blk(w, start):
        return pl.BlockSpec((SSD_CHUNK, w), lambda b, c: (b * nc + c, start // w))

    lane_pad = lambda v: jnp.pad(v.astype(F32), (0, LANES - v.shape[0])).reshape(1, LANES)
    params = [conv_w.astype(F32), conv_b.astype(F32).reshape(1, CONV_DIM), lane_pad(dt_bias), lane_pad(a_log),
              jnp.repeat(d_skip.astype(F32), P_B).reshape(1, D_INNER), norm_w.astype(F32).reshape(1, D_INNER)]
    in_specs = ([blk(GW_B, ABP_Z + g * GW_B) for g in range(NG_B)]
                + [blk(GW_B, ABP_XS + g * GW_B) for g in range(NG_B)]
                + [blk(NG_B * N_B, ABP_B), blk(NG_B * N_B, ABP_C), blk(LANES, ABP_DT)]
                + [_resident(a) for a in params])
    return pl.pallas_call(
        _ssd_kernel,
        out_shape=[jax.ShapeDtypeStruct((nseq * seqlen, D_INNER), out_dtype),
                   jax.ShapeDtypeStruct((nseq, NH_B, P_B, N_B), F32)],
        grid=(nseq, nc),
        in_specs=in_specs,
        out_specs=[pl.BlockSpec((SSD_CHUNK, D_INNER), lambda b, c: (b * nc + c, 0)),
                   pl.BlockSpec((None, NH_B, P_B, N_B), lambda b, c: (b, 0, 0, 0))],
        scratch_shapes=[pltpu.VMEM((NH_B * P_B, N_B), F32), pltpu.VMEM((SUBLANES, CONV_DIM), F32),
                        pltpu.VMEM((SSD_CHUNK, D_INNER), F32)],
        compiler_params=_cparams(2),
        name="ssd_chunk",
    )(p, *([p] * (2 * NG_B + 2)), *params)


def _softplus(x):
    return jnp.maximum(x, 0.0) + jnp.log1p(jnp.exp(-jnp.abs(x)))


def _ssd_step_kernel(*refs, ls):
    it = iter(refs)
    z_refs = [next(it) for _ in range(NG_B)]
    xs_refs = [next(it) for _ in range(NG_B)]
    b_ref, c_ref, dt_ref = next(it), next(it), next(it)
    cw_ref, cb_ref, dtb_ref, alog_ref, dexp_ref, nw_ref, e_ref, s0_ref, conv0_ref = (next(it) for _ in range(9))
    y_ref, sout_ref = refs[-2:]
    nt = (((1,), (1,)), ((), ()))

    def conv_silu(u, off):
        w = u.shape[1]
        r8 = lax.broadcasted_iota(jnp.int32, (ls, w), 0)
        pv = conv0_ref[:, off:off + w]
        cw = cw_ref[:, off:off + w]
        acc = u * cw[CONV_W - 1:CONV_W] + cb_ref[:, off:off + w]
        for k in range(1, CONV_W):
            sh = jnp.where(r8 < k, pltpu.roll(pv, k, axis=0), pltpu.roll(u, k, axis=0))
            acc = acc + sh * cw[CONV_W - 1 - k:CONV_W - k]
        return _silu(acc)

    row = lax.broadcasted_iota(jnp.int32, (ls, LANES), 0)
    lane = lax.broadcasted_iota(jnp.int32, (ls, LANES), 1)
    dt = _softplus(dt_ref[...] + dtb_ref[...])
    acum = dt * (-jnp.exp(alog_ref[...]))
    s = 1
    while s < ls:
        acum = acum + jnp.where(row >= s, pltpu.roll(acum, s, axis=0), 0.0)
        s *= 2
    a_last = acum[ls - 1:ls, :]
    ea = jnp.exp(acum)
    de = jnp.exp(a_last - acum) * dt
    cd = jnp.exp(a_last)

    bc = conv_silu(b_ref[...], D_INNER)
    cc = conv_silu(c_ref[...], D_INNER + NG_B * N_B)
    xs = jnp.concatenate([conv_silu(xs_refs[g][...], g * GW_B) for g in range(NG_B)], axis=1)
    bgs = [bc[:, g * N_B:(g + 1) * N_B] for g in range(NG_B)]
    cgs = [cc[:, g * N_B:(g + 1) * N_B].astype(BF16) for g in range(NG_B)]
    cbs = [lax.dot_general(cgs[g], bgs[g].astype(BF16), nt, preferred_element_type=F32) for g in range(NG_B)]

    pieces = []
    for s in range(ls):
        cbx = jnp.broadcast_to(cbs[NG_B - 1][:, s:s + 1], (ls, LANES))
        for g in range(NG_B - 2, -1, -1):
            cbx = jnp.where(lane < (g + 1) * HPG_B, cbs[g][:, s:s + 1], cbx)
        dec = jnp.exp(jnp.where(row >= s, acum - acum[s:s + 1, :], NEG_INF))
        pieces.append(cbx * dec * dt[s:s + 1, :])

    def hi_lo(v):
        hi = v.astype(BF16).astype(F32)
        return [hi, v - hi]

    spread = jnp.dot(jnp.concatenate(pieces + hi_lo(ea) + hi_lo(de), axis=0).astype(BF16), e_ref[...],
                     preferred_element_type=F32)
    yd = spread[0:ls] * xs[0:1, :]
    for s in range(1, ls):
        yd = yd + spread[s * ls:(s + 1) * ls] * xs[s:s + 1, :]
    ea_x = spread[ls * ls:ls * ls + ls] + spread[ls * ls + ls:ls * ls + 2 * ls]
    de_x = spread[ls * ls + 2 * ls:ls * ls + 3 * ls] + spread[ls * ls + 3 * ls:ls * ls + 4 * ls]

    yo = jnp.concatenate(
        [lax.dot_general(cgs[g], s0_ref[g * HPG_B:(g + 1) * HPG_B].reshape(GW_B, N_B).astype(BF16), nt,
                         preferred_element_type=F32) for g in range(NG_B)], axis=1)
    y = yd + yo * ea_x + dexp_ref[...] * xs
    xt = xs * de_x
    zero_rows = SSD_CHUNK - ls
    for g in range(NG_B):
        gs = slice(g * GW_B, (g + 1) * GW_B)
        gt = y[:, gs] * _silu(z_refs[g][...])
        y_ref[:, gs] = gt * lax.rsqrt(jnp.mean(gt * gt, axis=-1, keepdims=True) + RMS_EPS) * nw_ref[:, gs]
        xt_t = jnp.concatenate([xt[:, gs], jnp.zeros((zero_rows, GW_B), F32)], axis=0).T.astype(BF16)
        bg = jnp.concatenate([bgs[g], jnp.zeros((zero_rows, N_B), F32)], axis=0).astype(BF16)
        cs = jnp.dot(xt_t, bg, preferred_element_type=F32)
        for hh in range(HPG_B):
            h = g * HPG_B + hh
            sout_ref[h] = cd[:, h:h + 1] * s0_ref[h] + cs[hh * P_B:(hh + 1) * P_B]


def _ssd_step(p, conv_w, conv_b, dt_bias, a_log, d_skip, norm_w, init_state, init_conv, layer, prev_out, *, row0,
              nseq, ls):
    assert ls == SUBLANES and row0 % ls == 0
    r0 = row0 // ls

    def blk(w, start):
        return pl.BlockSpec((ls, w), lambda b: (r0 + b, start // w))

    lane_pad = lambda v: jnp.pad(v.astype(F32), (0, LANES - v.shape[0])).reshape(1, LANES)
    spread = (jnp.arange(LANES)[:, None] == jnp.arange(D_INNER)[None, :] // P_B).astype(BF16)
    params = [conv_w.astype(F32), conv_b.astype(F32).reshape(1, CONV_DIM), lane_pad(dt_bias), lane_pad(a_log),
              jnp.repeat(d_skip.astype(F32), P_B).reshape(1, D_INNER), norm_w.astype(F32).reshape(1, D_INNER), spread]
    state_spec = pl.BlockSpec((None, None, NH_B, P_B, N_B), lambda b: (layer, b, 0, 0, 0))
    in_specs = ([blk(GW_B, ABP_Z + g * GW_B) for g in range(NG_B)]
                + [blk(GW_B, ABP_XS + g * GW_B) for g in range(NG_B)]
                + [blk(NG_B * N_B, ABP_B), blk(NG_B * N_B, ABP_C), blk(LANES, ABP_DT)]
                + [_resident(a) for a in params]
                + [state_spec, pl.BlockSpec((None, SUBLANES, CONV_DIM), lambda b: (b, 0, 0))])
    args = [p] * (2 * NG_B + 3) + params + [
        init_state, jnp.pad(init_conv.astype(F32), ((0, 0), (SUBLANES - (CONV_W - 1), 0), (0, 0)))]
    aliases = {}
    if prev_out is not None:
        aliases = {len(args): 1}
        in_specs.append(pl.BlockSpec(memory_space=pl.ANY))
        args.append(prev_out)
    return pl.pallas_call(
        functools.partial(_ssd_step_kernel, ls=ls),
        out_shape=[jax.ShapeDtypeStruct((nseq * ls, D_INNER), F32), jax.ShapeDtypeStruct(init_state.shape, F32)],
        grid=(nseq,),
        in_specs=in_specs,
        out_specs=[pl.BlockSpec((ls, D_INNER), lambda b: (b, 0)), state_spec],
        input_output_aliases=aliases,
        compiler_params=_cparams(1),
        name="ssd_step",
    )(*args)


def _two_source(np_tiles, body, p_refs, s_refs):
    i = pl.program_id(0)

    @pl.when(i < np_tiles)
    def _():
        body(*[r[...] for r in p_refs])

    @pl.when(i >= np_tiles)
    def _():
        body(*[r[...] for r in s_refs])


def _ab_out_kernel(ap_ref, yp_ref, as_ref, ys_ref, wa_ref, wy_ref, x_ref, o_ref, *, np_tiles):
    def body(att, y):
        o_ref[...] = (x_ref[...] + jnp.dot(att.astype(BF16), wa_ref[...], preferred_element_type=F32)
                      + jnp.dot(y.astype(BF16), wy_ref[...], preferred_element_type=F32))

    _two_source(np_tiles, body, (ap_ref, yp_ref), (as_ref, ys_ref))


def _c_out_kernel(*refs, np_tiles):
    p_refs, s_refs = refs[0:6], refs[6:12]
    w_ref, x_ref, o_ref = refs[12:]

    def body(o1, o2, o3, l1, l2, l3):
        l1, l2, l3 = (l if l.shape[1] == C_OUT else _unpack_heads(l, HPG_C) for l in (l1, l2, l3))
        m = jnp.maximum(jnp.maximum(l1, l2), l3)
        e1, e2, e3 = jnp.exp(l1 - m), jnp.exp(l2 - m), jnp.exp(l3 - m)
        o = (e1 * o1.astype(F32) + e2 * o2.astype(F32) + e3 * o3.astype(F32)) * (1.0 / (e1 + e2 + e3))
        o_ref[...] = x_ref[...] + jnp.dot(o.astype(BF16), w_ref[...], preferred_element_type=F32)

    _two_source(np_tiles, body, p_refs, s_refs)


def _out_proj(kern, prompt_arrs, sample_arrs, weights, x, name):
    t, d = x.shape
    tp, ts = prompt_arrs[0].shape[0], sample_arrs[0].shape[0]
    assert tp + ts == t
    tm = _pick(math.gcd(tp, ts), (512, 256, 128))
    npt = tp // tm
    in_specs = ([pl.BlockSpec((tm, a.shape[1]), lambda i: (jnp.minimum(i, npt - 1), 0)) for a in prompt_arrs]
                + [pl.BlockSpec((tm, a.shape[1]), lambda i: (jnp.maximum(i - npt, 0), 0)) for a in sample_arrs]
                + [pl.BlockSpec(w.shape, lambda i: (0, 0)) for w in weights]
                + [pl.BlockSpec((tm, d), lambda i: (i, 0))])
    return pl.pallas_call(
        functools.partial(kern, np_tiles=npt),
        out_shape=jax.ShapeDtypeStruct((t, d), F32),
        grid=(t // tm,),
        in_specs=in_specs,
        out_specs=pl.BlockSpec((tm, d), lambda i: (i, 0)),
        compiler_params=_cparams(1),
        name=name,
    )(*prompt_arrs, *sample_arrs, *weights, x)


def kernel(x_prompt, x_sample, cache_a_k, cache_a_v, state_b_ssm, state_b_conv, cache_c1_k, cache_c1_v,
           cache_c2_k, cache_c2_v, cache_c3_k, cache_c3_v, rel_bias, norm_ff1, norm_mix, norm_ff2,
           norm_final, ff1_gate, ff1_up, ff1_down, ff2_gate, ff2_up, ff2_down, ab_w_in, ab_w_out, a_sinks,
           b_conv_w, b_conv_b, b_dt_bias, b_a_log, b_d, b_norm_w, c_w_in, c_w_out):
    bp, lp, d = x_prompt.shape
    bs, ls, _ = x_sample.shape
    tp, ts = bp * lp, bs * ls
    c_cache = ((cache_c1_k, cache_c1_v), (cache_c2_k, cache_c2_v), (cache_c3_k, cache_c3_v))
    x = jnp.concatenate([x_prompt.reshape(tp, d), x_sample.reshape(ts, d)], axis=0)

    def tail_rows(p, keep, c0, w):
        return jnp.stack([lax.slice(p, (s * lp + lp - keep, c0), ((s + 1) * lp, c0 + w)) for s in range(bp)])

    a_bias_p = _band_bias(rel_bias[:, :HQ_A], 1, WIN_A)
    ab_p, ab_s, c_p = [], [], []
    ssm_s = None
    c_rows = [[c.reshape(c.shape[0], bs, c.shape[2] * HPG_C, DH_C) for c in kv] for kv in c_cache]
    c_new = [None] * len(C_GROUPS)
    for layer in range(DEPTH):
        x = _ffn(x, norm_ff1[layer], ff1_gate[layer].astype(BF16), ff1_up[layer].astype(BF16),
                 ff1_down[layer].astype(BF16))
        if layer % 2 == 0:
            e = layer // 2
            w = ab_w_in[e]
            xbc0 = AB_Q + 2 * AB_KV + D_INNER
            w_in = jnp.concatenate([w[:, :AB_Q + 2 * AB_KV], w[:, xbc0 + CONV_DIM:],
                                    jnp.zeros((d, ABP_Z - ABP_DT - NH_B), w.dtype),
                                    w[:, AB_Q + 2 * AB_KV:xbc0], w[:, xbc0:xbc0 + CONV_DIM]], axis=1).astype(BF16)
            p = _rms_matmul(x, norm_mix[layer], w_in)
            ssd_params = (b_conv_w[e], b_conv_b[e], b_dt_bias[e], b_a_log[e], b_d[e], b_norm_w[e])

            att_p = _band_attn(p, a_bias_p, a_sinks[e], nseq=bp, seqlen=lp, hq=HQ_A, hkv=HKV_A, dh=DH_A,
                               q_start=ABP_Q, k_start=ABP_K, v_start=ABP_V, emit_lse=False, out_dtype=BF16)
            y_p, ssm_p = _ssd(p, *ssd_params, nseq=bp, seqlen=lp, out_dtype=BF16)

            lbuf = cache_a_k.shape[2]
            bias_c, bias_n = _step_bias(rel_bias[:, :HQ_A], lbuf, ls, 1, WIN_A, head_rows=False)
            sink_rows = jnp.repeat(a_sinks[e].astype(F32), ls).reshape(HQ_A * ls, 1)
            att_s, k_s, v_s = _step_attn(p, cache_a_k[e].reshape(bs, lbuf, AB_KV), cache_a_v[e].reshape(bs, lbuf, AB_KV),
                                         bias_c, bias_n, sink_rows, row0=tp, ls=ls, hq=HQ_A, hkv=HKV_A, dh=DH_A,
                                         q_start=ABP_Q, k_start=ABP_K, v_start=ABP_V, emit_lse=False)
            y_s, ssm_s = _ssd_step(p, *ssd_params, state_b_ssm, state_b_conv[e], e, ssm_s, row0=tp, nseq=bs, ls=ls)

            w_out = ab_w_out[e].astype(BF16)
            x = _out_proj(_ab_out_kernel, (att_p, y_p), (att_s, y_s), (w_out[:AB_Q], w_out[AB_Q:]), x, "ab_out")

            keep = min(WIN_A, lp)
            xbc_p = tail_rows(p, min(CONV_W - 1, lp), ABP_XS, CONV_DIM)
            xbc_s = p[tp:, ABP_XS:].reshape(bs, ls, CONV_DIM)
            ab_p.append((tail_rows(p, keep, ABP_K, AB_KV).reshape(bp, keep, HKV_A, DH_A),
                         tail_rows(p, keep, ABP_V, AB_KV).reshape(bp, keep, HKV_A, DH_A),
                         ssm_p,
                         jnp.concatenate([jnp.zeros((bp, CONV_W - 1, CONV_DIM), F32), xbc_p],
                                         axis=1)[:, -(CONV_W - 1):]))
            ab_s.append((k_s.reshape(bs, lbuf, HKV_A, DH_A), v_s.reshape(bs, lbuf, HKV_A, DH_A),
                         jnp.concatenate([state_b_conv[e], xbc_s], axis=1)[:, -(CONV_W - 1):]))
        else:
            o = layer // 2
            p = _rms_matmul(x, norm_mix[layer], c_w_in[o].astype(BF16))
            outs_p, lses_p, outs_s, lses_s, st_p = [], [], [], [], []
            for gi, (win, dil) in enumerate(C_GROUPS):
                n_win = win // dil
                bias_heads = rel_bias[:, gi * HPG_C:(gi + 1) * HPG_C]
                starts = dict(q_start=gi * C_OUT, k_start=(3 + gi) * C_OUT, v_start=(6 + gi) * C_OUT)
                if dil == 1:
                    o_p, l_p = _band_attn(p, _band_bias(bias_heads, dil, n_win), None, nseq=bp, seqlen=lp,
                                          hq=HPG_C, hkv=HPG_C, dh=DH_C, emit_lse=True, out_dtype=BF16, **starts)
                else:
                    o_p, l_p = _dil_attn(p, _band_bias(bias_heads, dil, n_win), nseq=bp, seqlen=lp, dil=dil,
                                         nh=HPG_C, dh=DH_C, **starts)
                lbuf = c_cache[gi][0].shape[2]
                bias_c, bias_n = _step_bias(bias_heads, lbuf, ls, dil, n_win, head_rows=True)
                o_s, l_s, k_s, v_s = _step_attn_rows(p, c_rows[gi][0], c_rows[gi][1], o, c_new[gi], bias_c, bias_n,
                                                     row0=tp, ls=ls, nh=HPG_C, dh=DH_C, **starts)
                c_new[gi] = (k_s, v_s)
                outs_p.append(o_p)
                lses_p.append(l_p)
                outs_s.append(o_s)
                lses_s.append(l_s)
                keep = min(win, lp)
                st_p += [tail_rows(p, keep, starts["k_start"], C_OUT).reshape(bp, keep, HPG_C, DH_C),
                         tail_rows(p, keep, starts["v_start"], C_OUT).reshape(bp, keep, HPG_C, DH_C)]
            x = _out_proj(_c_out_kernel, (*outs_p, *lses_p), (*outs_s, *lses_s), (c_w_out[o].astype(BF16),), x,
                          "c_out")
            c_p.append(tuple(st_p))
        x = _ffn(x, norm_ff2[layer], ff2_gate[layer].astype(BF16), ff2_up[layer].astype(BF16),
                 ff2_down[layer].astype(BF16), final_w=norm_final if layer == DEPTH - 1 else None)

    def stack(states):
        return [jnp.stack([s[i] for s in states]) for i in range(len(states[0]))]

    a_k_s, a_v_s, conv_s = stack(ab_s)
    c_s = [a.reshape(c.shape) for kv_new, kv in zip(c_new, c_cache) for a, c in zip(kv_new, kv)]
    return (x[:tp].reshape(bp, lp, d), x[tp:].reshape(bs, ls, d),
            *stack(ab_p), *stack(c_p), a_k_s, a_v_s, ssm_s, conv_s, *c_s)
```

```python
import functools
import math

import jax
import jax.numpy as jnp
from jax import lax
from jax.experimental import pallas as pl
from jax.experimental.pallas import tpu as pltpu

F32 = jnp.float32
BF16 = jnp.bfloat16

D_MODEL = 1024
DEPTH = 4
NUM_BUCKETS = 32
MAX_DISTANCE = 2048
Q_BLOCK = 128
RMS_EPS = 1e-6
NEG_INF = -1e30
HQ_A, HKV_A, DH_A, WIN_A = 12, 4, 64, 128
D_INNER = 2 * D_MODEL
P_B = 64
NH_B = D_INNER // P_B
N_B = 128
NG_B = 4
HPG_B = NH_B // NG_B
GW_B = HPG_B * P_B
CONV_W = 4
CONV_DIM = D_INNER + 2 * NG_B * N_B
SSD_CHUNK = 128
C_GROUPS = ((128, 1), (512, 4), (2048, 16))
HPG_C = 4
DH_C = 128
C_OUT = HPG_C * DH_C
AB_Q = HQ_A * DH_A
AB_KV = HKV_A * DH_A
C_IN = 3 * 3 * C_OUT

ABP_Q, ABP_K, ABP_V, ABP_DT = 0, AB_Q, AB_Q + AB_KV, AB_Q + 2 * AB_KV
ABP_Z = ABP_DT + 256
ABP_XS = ABP_Z + D_INNER
ABP_B = ABP_XS + D_INNER
ABP_C = ABP_B + NG_B * N_B
ABP_N = ABP_C + NG_B * N_B

SUBLANES = 8
LANES = 128
VMEM_LIMIT = 56 * 1024 * 1024


def _cparams(n_axes):
    return pltpu.CompilerParams(dimension_semantics=("arbitrary",) * n_axes, vmem_limit_bytes=VMEM_LIMIT)


def _pick(n, options):
    for o in options:
        if n % o == 0:
            return o
    raise ValueError(f"no tile in {options} divides {n}")


def _rms(x, w):
    ms = jnp.mean(x * x, axis=-1, keepdims=True)
    return x * lax.rsqrt(ms + RMS_EPS) * w


def _silu(x):
    return x * jax.nn.sigmoid(x)


def _resident(a):
    return pl.BlockSpec(a.shape, lambda *_: (0,) * a.ndim, pipeline_mode=pl.Buffered(1))


def _ffn_kernel(x_ref, nw_ref, wg_ref, wu_ref, wd_ref, fw_ref, o_ref, *, final_norm):
    x = x_ref[...]
    h = _rms(x, nw_ref[...]).astype(BF16)
    g = jnp.dot(h, wg_ref[...], preferred_element_type=F32)
    u = jnp.dot(h, wu_ref[...], preferred_element_type=F32)
    a = (_silu(g) * u).astype(BF16)
    y = x + 0.5 * jnp.dot(a, wd_ref[...], preferred_element_type=F32)
    if final_norm:
        y = _rms(y, fw_ref[...])
    o_ref[...] = y


def _ffn(x, nw, wg, wu, wd, final_w=None):
    t, d = x.shape
    tm = _pick(t, (512, 256, 128))
    fw = jnp.ones((d,), F32) if final_w is None else final_w
    args = (x, nw.reshape(1, d), wg, wu, wd, fw.reshape(1, d))
    return pl.pallas_call(
        functools.partial(_ffn_kernel, final_norm=final_w is not None),
        out_shape=jax.ShapeDtypeStruct((t, d), F32),
        grid=(t // tm,),
        in_specs=[pl.BlockSpec((tm, d), lambda i: (i, 0))] + [_resident(a) for a in args[1:]],
        out_specs=pl.BlockSpec((tm, d), lambda i: (i, 0)),
        compiler_params=_cparams(1),
        name="ffn",
    )(*args)


def _rms_matmul_kernel(x_ref, nw_ref, w_ref, o_ref):
    h = _rms(x_ref[...], nw_ref[...]).astype(BF16)
    o_ref[...] = jnp.dot(h, w_ref[...], preferred_element_type=F32)


def _rms_matmul(x, nw, w):
    t, d = x.shape
    n = w.shape[1]
    tm = _pick(t, (512, 256, 128))
    args = (x, nw.reshape(1, d), w)
    return pl.pallas_call(
        _rms_matmul_kernel,
        out_shape=jax.ShapeDtypeStruct((t, n), F32),
        grid=(t // tm,),
        in_specs=[pl.BlockSpec((tm, d), lambda i: (i, 0))] + [_resident(a) for a in args[1:]],
        out_specs=pl.BlockSpec((tm, n), lambda i: (i, 0)),
        compiler_params=_cparams(1),
        name="rms_matmul",
    )(*args)


def _t5_bucket(dist):
    n = jnp.maximum(dist, 0)
    max_exact = NUM_BUCKETS // 2
    nf = jnp.maximum(n, 1).astype(F32)
    large = max_exact + (jnp.log(nf / max_exact) / math.log(MAX_DISTANCE / max_exact)
                         * (NUM_BUCKETS - max_exact)).astype(jnp.int32)
    large = jnp.minimum(large, NUM_BUCKETS - 1)
    return jnp.where(n < max_exact, n, large)


def _bias_lookup(bias_heads, dist):
    onehot = jax.nn.one_hot(_t5_bucket(dist), NUM_BUCKETS, dtype=F32)
    return jnp.einsum('...k,kh->...h', onehot, bias_heads.astype(F32), precision=lax.Precision.HIGHEST)


def _band_bias(bias_heads, dil, n_win):
    qi = jnp.arange(Q_BLOCK)[:, None]
    si = jnp.arange(2 * Q_BLOCK)[None, :]
    dist = qi + Q_BLOCK - si
    valid = (dist >= 0) & (dist <= n_win)
    bias = jnp.transpose(_bias_lookup(bias_heads, dist * dil), (2, 0, 1))
    return jnp.where(valid[None], bias, NEG_INF)


def _step_bias(bias_heads, lbuf, ls, dil, n_win, head_rows):
    nh = bias_heads.shape[1]
    i = jnp.arange(ls)[:, None]
    eye = jnp.eye(nh, dtype=bool)

    def table(delta, head_major):
        ok = (delta >= 0) & (delta % dil == 0) & (delta // dil <= n_win)
        b = jnp.transpose(jnp.where(ok[..., None], _bias_lookup(bias_heads, delta), NEG_INF), (2, 0, 1))
        if head_rows and head_major:
            b = jnp.where(eye[:, None, :, None], b[:, :, None, :], NEG_INF)
        elif head_rows:
            b = jnp.where(eye[:, None, None, :], b[:, :, :, None], NEG_INF)
        return b.reshape(nh * ls, -1)

    bias_c = table(lbuf + i - jnp.arange(lbuf)[None, :], False)
    bias_n = table(i - jnp.arange(ls)[None, :], True)
    bias_n = jnp.pad(bias_n, ((0, 0), (0, LANES - bias_n.shape[1])), constant_values=NEG_INF)
    return bias_c, bias_n


def _band_attn_kernel(*refs, hq, hkv, dh, has_sink, emit_lse):
    it = iter(refs)
    q_ref, kp_ref, kc_ref, vp_ref, vc_ref, bias_ref = (next(it) for _ in range(6))
    sink_ref = next(it) if has_sink else None
    o_ref = next(it)
    lse_ref = next(it) if emit_lse else None
    g = hq // hkv
    scale = dh ** -0.5
    col = lax.broadcasted_iota(jnp.int32, (Q_BLOCK, 2 * Q_BLOCK), 1)
    no_prev = col < jnp.where(pl.program_id(1) == 0, Q_BLOCK, 0)
    outs, lses = [], []
    for kv in range(hkv):
        sl = slice(kv * dh, (kv + 1) * dh)
        k = jnp.concatenate([kp_ref[:, sl], kc_ref[:, sl]], axis=0).astype(BF16)
        v = jnp.concatenate([vp_ref[:, sl], vc_ref[:, sl]], axis=0).astype(BF16)
        for gi in range(g):
            h = kv * g + gi
            qh = (q_ref[:, h * dh:(h + 1) * dh] * scale).astype(BF16)
            s = lax.dot_general(qh, k, (((1,), (1,)), ((), ())), preferred_element_type=F32)
            s = jnp.where(no_prev, NEG_INF, s + bias_ref[h])
            m = jnp.max(s, axis=-1, keepdims=True)
            if has_sink:
                m = jnp.maximum(m, sink_ref[h])
            p = jnp.exp(s - m)
            l = jnp.sum(p, axis=-1, keepdims=True)
            if has_sink:
                l = l + jnp.exp(sink_ref[h] - m)
            o = jnp.dot(p.astype(BF16), v, preferred_element_type=F32)
            outs.append(o * (1.0 / l))
            lses.append(m + jnp.log(l))
    o_ref[...] = jnp.concatenate(outs, axis=1).astype(o_ref.dtype)
    if emit_lse:
        lse_ref[...] = _pack_heads(lses)


def _pack_heads(cols):
    n = len(cols)
    rows = cols[0].shape[0]
    lane = lax.broadcasted_iota(jnp.int32, (rows, LANES), 1)
    out = jnp.broadcast_to(cols[-1], (rows, LANES))
    for h in range(n - 2, -1, -1):
        out = jnp.where(lane < (h + 1) * (LANES // n), cols[h], out)
    return out


def _unpack_heads(packed, n):
    rows = packed.shape[0]
    w = LANES // n
    return jnp.concatenate([jnp.broadcast_to(packed[:, h * w:h * w + 1], (rows, LANES)) for h in range(n)], axis=1)


def _band_attn(p, bias, sinks, *, nseq, seqlen, hq, hkv, dh, q_start, k_start, v_start, emit_lse, out_dtype):
    wq, wk = hq * dh, hkv * dh
    assert seqlen % Q_BLOCK == 0
    assert q_start % wq == 0 and k_start % wk == 0 and v_start % wk == 0
    nb = seqlen // Q_BLOCK

    def cur(start, w):
        return lambda b, n: (b * nb + n, start // w)

    def prev(start, w):
        return lambda b, n: (b * nb + jnp.maximum(n - 1, 0), start // w)

    in_specs = [
        pl.BlockSpec((Q_BLOCK, wq), cur(q_start, wq)),
        pl.BlockSpec((Q_BLOCK, wk), prev(k_start, wk)),
        pl.BlockSpec((Q_BLOCK, wk), cur(k_start, wk)),
        pl.BlockSpec((Q_BLOCK, wk), prev(v_start, wk)),
        pl.BlockSpec((Q_BLOCK, wk), cur(v_start, wk)),
        pl.BlockSpec((hq, Q_BLOCK, 2 * Q_BLOCK), lambda b, n: (0, 0, 0)),
    ]
    args = [p, p, p, p, p, bias]
    if sinks is not None:
        in_specs.append(pl.BlockSpec(memory_space=pltpu.SMEM))
        args.append(sinks.astype(F32))
    out_shape = [jax.ShapeDtypeStruct((nseq * seqlen, wq), out_dtype)]
    out_specs = [pl.BlockSpec((Q_BLOCK, wq), lambda b, n: (b * nb + n, 0))]
    if emit_lse:
        out_shape.append(jax.ShapeDtypeStruct((nseq * seqlen, LANES), F32))
        out_specs.append(pl.BlockSpec((Q_BLOCK, LANES), lambda b, n: (b * nb + n, 0)))
    res = pl.pallas_call(
        functools.partial(_band_attn_kernel, hq=hq, hkv=hkv, dh=dh, has_sink=sinks is not None, emit_lse=emit_lse),
        out_shape=out_shape,
        grid=(nseq, nb),
        in_specs=in_specs,
        out_specs=out_specs,
        compiler_params=_cparams(2),
        name="band_attn",
    )(*args)
    return tuple(res) if emit_lse else res[0]


def _dil_attn_kernel(q_ref, kp_ref, kc_ref, vp_ref, vc_ref, bias_ref, o_ref, lse_ref, *, dil, dh):
    h = pl.program_id(2)
    col = lax.broadcasted_iota(jnp.int32, (Q_BLOCK, 2 * Q_BLOCK), 1)
    no_prev = col < jnp.where(pl.program_id(1) == 0, Q_BLOCK, 0)
    bias = bias_ref[h]
    scale = dh ** -0.5
    for r in range(dil):
        rows = pl.ds(r, Q_BLOCK, stride=dil)
        qh = (q_ref[rows, :] * scale).astype(BF16)
        k = jnp.concatenate([kp_ref[rows, :], kc_ref[rows, :]], axis=0).astype(BF16)
        v = jnp.concatenate([vp_ref[rows, :], vc_ref[rows, :]], axis=0).astype(BF16)
        s = lax.dot_general(qh, k, (((1,), (1,)), ((), ())), preferred_element_type=F32)
        s = jnp.where(no_prev, NEG_INF, s + bias)
        m = jnp.max(s, axis=-1, keepdims=True)
        p = jnp.exp(s - m)
        l = jnp.sum(p, axis=-1, keepdims=True)
        o = jnp.dot(p.astype(BF16), v, preferred_element_type=F32)
        o_ref[rows, :] = o * (1.0 / l)
        lse_ref[rows, :] = jnp.broadcast_to(m + jnp.log(l), (Q_BLOCK, dh))


def _dil_attn(p, bias, *, nseq, seqlen, dil, nh, dh, q_start, k_start, v_start):
    sb = dil * Q_BLOCK
    assert seqlen % sb == 0 and dh == LANES
    assert q_start % dh == 0 and k_start % dh == 0 and v_start % dh == 0
    nsb = seqlen // sb

    def cur(start):
        return pl.BlockSpec((sb, dh), lambda b, n, h: (b * nsb + n, start // dh + h))

    def prev(start):
        return pl.BlockSpec((sb, dh), lambda b, n, h: (b * nsb + jnp.maximum(n - 1, 0), start // dh + h))

    out_spec = pl.BlockSpec((sb, dh), lambda b, n, h: (b * nsb + n, h))
    return pl.pallas_call(
        functools.partial(_dil_attn_kernel, dil=dil, dh=dh),
        out_shape=[jax.ShapeDtypeStruct((nseq * seqlen, nh * dh), F32)] * 2,
        grid=(nseq, nsb, nh),
        in_specs=[cur(q_start), prev(k_start), cur(k_start), prev(v_start), cur(v_start),
                  pl.BlockSpec((nh, Q_BLOCK, 2 * Q_BLOCK), lambda b, n, h: (0, 0, 0))],
        out_specs=[out_spec, out_spec],
        compiler_params=_cparams(3),
        name=f"dil_attn_d{dil}",
    )(p, p, p, p, p, bias)


def _step_attn_kernel(*refs, ls, hq, hkv, dh, has_sink, emit_lse):
    it = iter(refs)
    q_ref, kn_ref, vn_ref, ck_ref, cv_ref, bc_ref, bn_ref = (next(it) for _ in range(7))
    sink_ref = next(it) if has_sink else None
    o_ref = next(it)
    lse_ref = next(it) if emit_lse else None
    ok_ref, ov_ref = next(it), next(it)
    lbuf = ck_ref.shape[1]
    w = hkv * dh
    g = hq // hkv
    kn, vn = kn_ref[...], vn_ref[...]

    ok_ref[0, :lbuf - ls, :] = ck_ref[0, ls:, :]
    ok_ref[0, lbuf - ls:, :] = kn
    ov_ref[0, :lbuf - ls, :] = cv_ref[0, ls:, :]
    ov_ref[0, lbuf - ls:, :] = vn

    q = q_ref[...] * (dh ** -0.5)
    if g == 1:
        lane = lax.broadcasted_iota(jnp.int32, (ls, w), 1)
        head_of_lane = [(lane >= h * dh) & (lane < (h + 1) * dh) for h in range(hq)]
        qbd = jnp.concatenate([jnp.where(head_of_lane[h], q, 0.0) for h in range(hq)], axis=0)
    else:
        blocks = []
        for h in range(hq):
            kv = h // g
            parts = []
            if kv > 0:
                parts.append(jnp.zeros((ls, kv * dh), F32))
            parts.append(q[:, h * dh:(h + 1) * dh])
            if kv < hkv - 1:
                parts.append(jnp.zeros((ls, (hkv - 1 - kv) * dh), F32))
            blocks.append(jnp.concatenate(parts, axis=1))
        qbd = jnp.concatenate(blocks, axis=0)
    qbd = qbd.astype(BF16)

    pad = jnp.zeros((LANES - ls, w), F32)
    knp = jnp.concatenate([kn, pad], axis=0).astype(BF16)
    vnp = jnp.concatenate([vn, pad], axis=0).astype(BF16)
    nt = (((1,), (1,)), ((), ()))
    sc = lax.dot_general(qbd, ck_ref[0].astype(BF16), nt, preferred_element_type=F32) + bc_ref[...]
    sn = lax.dot_general(qbd, knp, nt, preferred_element_type=F32) + bn_ref[...]
    m = jnp.maximum(jnp.max(sc, axis=-1, keepdims=True), jnp.max(sn, axis=-1, keepdims=True))
    if has_sink:
        m = jnp.maximum(m, sink_ref[...])
    pc = jnp.exp(sc - m)
    pn = jnp.exp(sn - m)
    l = jnp.sum(pc, axis=-1, keepdims=True) + jnp.sum(pn, axis=-1, keepdims=True)
    if has_sink:
        l = l + jnp.exp(sink_ref[...] - m)
    of = (jnp.dot(pc.astype(BF16), cv_ref[0].astype(BF16), preferred_element_type=F32)
          + jnp.dot(pn.astype(BF16), vnp, preferred_element_type=F32)) * (1.0 / l)

    if g == 1:
        o = jnp.where(head_of_lane[0], of[:ls], 0.0)
        for h in range(1, hq):
            o = o + jnp.where(head_of_lane[h], of[h * ls:(h + 1) * ls], 0.0)
    else:
        o = jnp.concatenate([of[h * ls:(h + 1) * ls, (h // g) * dh:(h // g + 1) * dh] for h in range(hq)], axis=1)
    o_ref[...] = o
    if emit_lse:
        lse = m + jnp.log(l)
        lse_ref[...] = _pack_heads([lse[h * ls:(h + 1) * ls] for h in range(hq)])


def _step_attn(p, cache_k, cache_v, bias_c, bias_n, sink_rows, *, row0, ls, hq, hkv, dh, q_start, k_start, v_start,
               emit_lse):
    bs, lbuf, w = cache_k.shape
    wq = hq * dh
    assert ls == SUBLANES and row0 % ls == 0 and w == hkv * dh and lbuf % ls == 0
    assert q_start % wq == 0 and k_start % w == 0 and v_start % w == 0
    r0 = row0 // ls
    in_specs = [
        pl.BlockSpec((ls, wq), lambda b: (r0 + b, q_start // wq)),
        pl.BlockSpec((ls, w), lambda b: (r0 + b, k_start // w)),
        pl.BlockSpec((ls, w), lambda b: (r0 + b, v_start // w)),
        pl.BlockSpec((1, lbuf, w), lambda b: (b, 0, 0)),
        pl.BlockSpec((1, lbuf, w), lambda b: (b, 0, 0)),
        pl.BlockSpec((hq * ls, lbuf), lambda b: (0, 0)),
        pl.BlockSpec((hq * ls, LANES), lambda b: (0, 0)),
    ]
    args = [p, p, p, cache_k, cache_v, bias_c, bias_n]
    if sink_rows is not None:
        in_specs.append(pl.BlockSpec((hq * ls, 1), lambda b: (0, 0)))
        args.append(sink_rows)
    out_shape = [jax.ShapeDtypeStruct((bs * ls, wq), F32)]
    out_specs = [pl.BlockSpec((ls, wq), lambda b: (b, 0))]
    if emit_lse:
        out_shape.append(jax.ShapeDtypeStruct((bs * ls, LANES), F32))
        out_specs.append(pl.BlockSpec((ls, LANES), lambda b: (b, 0)))
    out_shape += [jax.ShapeDtypeStruct((bs, lbuf, w), F32)] * 2
    out_specs += [pl.BlockSpec((1, lbuf, w), lambda b: (b, 0, 0))] * 2
    return pl.pallas_call(
        functools.partial(_step_attn_kernel, ls=ls, hq=hq, hkv=hkv, dh=dh, has_sink=sink_rows is not None,
                          emit_lse=emit_lse),
        out_shape=out_shape,
        grid=(bs,),
        in_specs=in_specs,
        out_specs=out_specs,
        compiler_params=_cparams(1),
        name=f"step_attn_l{lbuf}",
    )(*args)


def _step_attn_rows_kernel(*refs, ls, nh, dh, aliased):
    it = iter(refs)
    q_ref, kn_ref, vn_ref, ck_ref, cv_ref, bc_ref, bn_ref = (next(it) for _ in range(7))
    if aliased:
        next(it), next(it)
    o_ref, lse_ref, ok_ref, ov_ref = (next(it) for _ in range(4))
    nseq, rows = ck_ref.shape[0], ck_ref.shape[1]
    nnew = ls * nh
    heads = [slice(h * dh, (h + 1) * dh) for h in range(nh)]
    nt = (((1,), (1,)), ((), ()))
    pad = [jnp.zeros((LANES - nnew, dh), F32)]
    for j in range(nseq):
        tok = slice(j * ls, (j + 1) * ls)
        kn, vn = kn_ref[tok, :], vn_ref[tok, :]

        ok_ref[j, :rows - nnew, :] = ck_ref[j, nnew:, :]
        ov_ref[j, :rows - nnew, :] = cv_ref[j, nnew:, :]
        for i in range(ls):
            for h in range(nh):
                r = rows - nnew + i * nh + h
                ok_ref[j, r:r + 1, :] = kn[i:i + 1, heads[h]]
                ov_ref[j, r:r + 1, :] = vn[i:i + 1, heads[h]]

        q = q_ref[tok, :] * (dh ** -0.5)
        qr = jnp.concatenate([q[:, hs] for hs in heads], axis=0).astype(BF16)
        knr = jnp.concatenate([kn[:, hs] for hs in heads] + pad, axis=0).astype(BF16)
        vnr = jnp.concatenate([vn[:, hs] for hs in heads] + pad, axis=0).astype(BF16)
        sc = lax.dot_general(qr, ck_ref[j].astype(BF16), nt, preferred_element_type=F32) + bc_ref[...]
        sn = lax.dot_general(qr, knr, nt, preferred_element_type=F32) + bn_ref[...]
        m = jnp.maximum(jnp.max(sc, axis=-1, keepdims=True), jnp.max(sn, axis=-1, keepdims=True))
        pc = jnp.exp(sc - m)
        pn = jnp.exp(sn - m)
        l = jnp.sum(pc, axis=-1, keepdims=True) + jnp.sum(pn, axis=-1, keepdims=True)
        of = (jnp.dot(pc.astype(BF16), cv_ref[j].astype(BF16), preferred_element_type=F32)
              + jnp.dot(pn.astype(BF16), vnr, preferred_element_type=F32)) * (1.0 / l)
        o_ref[tok, :] = jnp.concatenate([of[h * ls:(h + 1) * ls] for h in range(nh)], axis=1)
        lse = m + jnp.log(l)
        lse_ref[tok, :] = _pack_heads([lse[h * ls:(h + 1) * ls] for h in range(nh)])


def _step_attn_rows(p, cache_k, cache_v, layer, prev_out, bias_c, bias_n, *, row0, ls, nh, dh, q_start, k_start,
                    v_start):
    nl, bs, rows, _ = cache_k.shape
    w = nh * dh
    assert ls == SUBLANES and dh == LANES and rows % (ls * nh) == 0
    assert q_start % w == 0 and k_start % w == 0 and v_start % w == 0
    nb = _pick(bs, [n for n in (8, 4, 2, 1) if n * rows <= 4096 or n == 1])
    tok = nb * ls
    assert row0 % tok == 0
    r0 = row0 // tok
    cache_spec = pl.BlockSpec((None, nb, rows, dh), lambda b: (layer, b, 0, 0))
    in_specs = [
        pl.BlockSpec((tok, w), lambda b: (r0 + b, q_start // w)),
        pl.BlockSpec((tok, w), lambda b: (r0 + b, k_start // w)),
        pl.BlockSpec((tok, w), lambda b: (r0 + b, v_start // w)),
        cache_spec, cache_spec,
        pl.BlockSpec((nh * ls, rows), lambda b: (0, 0)),
        pl.BlockSpec((nh * ls, LANES), lambda b: (0, 0)),
    ]
    args = [p, p, p, cache_k, cache_v, bias_c, bias_n]
    aliases = {}
    if prev_out is not None:
        in_specs += [pl.BlockSpec(memory_space=pl.ANY)] * 2
        args += list(prev_out)
        aliases = {7: 2, 8: 3}
    return pl.pallas_call(
        functools.partial(_step_attn_rows_kernel, ls=ls, nh=nh, dh=dh, aliased=prev_out is not None),
        out_shape=[jax.ShapeDtypeStruct((bs * ls, w), F32), jax.ShapeDtypeStruct((bs * ls, LANES), F32),
                   jax.ShapeDtypeStruct(cache_k.shape, F32), jax.ShapeDtypeStruct(cache_v.shape, F32)],
        grid=(bs // nb,),
        in_specs=in_specs,
        out_specs=[pl.BlockSpec((tok, w), lambda b: (b, 0)), pl.BlockSpec((tok, LANES), lambda b: (b, 0)),
                   cache_spec, cache_spec],
        input_output_aliases=aliases,
        compiler_params=_cparams(1),
        name=f"step_attn_rows{rows}",
    )(*args)


def _ssd_kernel(*refs):
    it = iter(refs)
    z_refs = [next(it) for _ in range(NG_B)]
    xs_refs = [next(it) for _ in range(NG_B)]
    b_ref, c_ref, dt_ref = next(it), next(it), next(it)
    cw_ref, cb_ref, dtb_ref, alog_ref, dexp_ref, nw_ref = (next(it) for _ in range(6))
    y_ref, sout_ref, state, prev, yscr = refs[-5:]
    cl = SSD_CHUNK
    c = pl.program_id(1)

    @pl.when(c == 0)
    def _():
        state[...] = jnp.zeros_like(state)
        prev[...] = jnp.zeros_like(prev)

    def conv_silu(u, off):
        w = u.shape[1]
        pv = prev[:, off:off + w]
        cw = cw_ref[:, off:off + w]
        acc = u * cw[CONV_W - 1:CONV_W] + cb_ref[:, off:off + w]
        r8 = lax.broadcasted_iota(jnp.int32, (SUBLANES, w), 0)
        for k in range(1, CONV_W):
            rolled = pltpu.roll(u, k, axis=0)
            top = jnp.where(r8 < k, pltpu.roll(pv, k, axis=0), rolled[:SUBLANES])
            sh = jnp.concatenate([top, rolled[SUBLANES:]], axis=0)
            acc = acc + sh * cw[CONV_W - 1 - k:CONV_W - k]
        prev[:, off:off + w] = u[cl - SUBLANES:]
        return _silu(acc)

    row = lax.broadcasted_iota(jnp.int32, (cl, LANES), 0)
    col = lax.broadcasted_iota(jnp.int32, (cl, LANES), 1)
    dt = _softplus(dt_ref[...] + dtb_ref[...])
    acum = dt * (-jnp.exp(alog_ref[...]))
    s = 1
    while s < cl:
        acum = acum + jnp.where(row >= s, pltpu.roll(acum, s, axis=0), 0.0)
        s *= 2
    a_last = acum[cl - 1:cl, :]
    ea = jnp.exp(acum)
    de = jnp.exp(a_last - acum) * dt
    cd = jnp.exp(a_last)
    ac_t, dt_t, de_t = acum.T, dt.T, de.T
    tri = row >= col
    first = col < P_B
    first_rows = row < P_B
    pair = 2 * P_B

    bc = conv_silu(b_ref[...], D_INNER).astype(BF16)
    cc = conv_silu(c_ref[...], D_INNER + NG_B * N_B).astype(BF16)
    nn = (((1,), (0,)), ((), ()))
    nt = (((1,), (1,)), ((), ()))
    for g in range(NG_B):
        bg = bc[:, g * N_B:(g + 1) * N_B]
        cg = cc[:, g * N_B:(g + 1) * N_B]
        cbm = lax.dot_general(cg, bg, nt, preferred_element_type=F32)
        xg = conv_silu(xs_refs[g][...], g * GW_B)
        xg_t = xg.T
        for pr in range(HPG_B // 2):
            h1 = g * HPG_B + 2 * pr
            h2 = h1 + 1
            ps = slice(pr * pair, (pr + 1) * pair)
            cs = slice(g * GW_B + pr * pair, g * GW_B + (pr + 1) * pair)
            mms = []
            for h in (h1, h2):
                dec = jnp.exp(jnp.where(tri, acum[:, h:h + 1] - ac_t[h:h + 1, :], NEG_INF))
                mms.append((cbm * dec * dt_t[h:h + 1, :]).astype(BF16))
            xp = xg[:, ps]
            xbd = jnp.concatenate([jnp.where(first, xp, 0.0).astype(BF16), jnp.where(first, 0.0, xp).astype(BF16)],
                                  axis=0)
            yd = lax.dot_general(jnp.concatenate(mms, axis=1), xbd, nn, preferred_element_type=F32)
            sp = state[h1 * P_B:(h2 + 1) * P_B, :]
            yo = lax.dot_general(cg, sp.astype(BF16), nt, preferred_element_type=F32)
            ea2 = jnp.where(first, ea[:, h1:h1 + 1], ea[:, h2:h2 + 1])
            yscr[:, cs] = yd + yo * ea2 + dexp_ref[:, cs] * xp
            de2 = jnp.where(first_rows, de_t[h1:h1 + 1, :], de_t[h2:h2 + 1, :])
            cd2 = jnp.where(first_rows, cd[:, h1:h1 + 1], cd[:, h2:h2 + 1])
            xt = (xg_t[ps, :] * de2).astype(BF16)
            state[h1 * P_B:(h2 + 1) * P_B, :] = cd2 * sp + lax.dot_general(xt, bg, nn, preferred_element_type=F32)
        gs = slice(g * GW_B, (g + 1) * GW_B)
        gt = yscr[:, gs] * _silu(z_refs[g][...])
        gt = gt * lax.rsqrt(jnp.mean(gt * gt, axis=-1, keepdims=True) + RMS_EPS) * nw_ref[:, gs]
        y_ref[:, gs] = gt.astype(y_ref.dtype)

    @pl.when(c == pl.num_programs(1) - 1)
    def _():
        sout_ref[...] = state[...].reshape(NH_B, P_B, N_B)


def _ssd(p, conv_w, conv_b, dt_bias, a_log, d_skip, norm_w, *, nseq, seqlen, out_dtype):
    assert seqlen % SSD_CHUNK == 0
    nc = seqlen // SSD_CHUNK

    def blk(w, start):
        return pl.BlockSpec((SSD_CHUNK, w), lambda b, c: (b * nc + c, start // w))

    lane_pad = lambda v: jnp.pad(v.astype(F32), (0, LANES - v.shape[0])).reshape(1, LANES)
    params = [conv_w.astype(F32), conv_b.astype(F32).reshape(1, CONV_DIM), lane_pad(dt_bias), lane_pad(a_log),
              jnp.repeat(d_skip.astype(F32), P_B).reshape(1, D_INNER), norm_w.astype(F32).reshape(1, D_INNER)]
    in_specs = ([blk(GW_B, ABP_Z + g * GW_B) for g in range(NG_B)]
                + [blk(GW_B, ABP_XS + g * GW_B) for g in range(NG_B)]
                + [blk(NG_B * N_B, ABP_B), blk(NG_B * N_B, ABP_C), blk(LANES, ABP_DT)]
                + [_resident(a) for a in params])
    return pl.pallas_call(
        _ssd_kernel,
        out_shape=[jax.ShapeDtypeStruct((nseq * seqlen, D_INNER), out_dtype),
                   jax.ShapeDtypeStruct((nseq, NH_B, P_B, N_B), F32)],
        grid=(nseq, nc),
        in_specs=in_specs,
        out_specs=[pl.BlockSpec((SSD_CHUNK, D_INNER), lambda b, c: (b * nc + c, 0)),
                   pl.BlockSpec((None, NH_B, P_B, N_B), lambda b, c: (b, 0, 0, 0))],
        scratch_shapes=[pltpu.VMEM((NH_B * P_B, N_B), F32), pltpu.VMEM((SUBLANES, CONV_DIM), F32),
                        pltpu.VMEM((SSD_CHUNK, D_INNER), F32)],
        compiler_params=_cparams(2),
        name="ssd_chunk",
    )(p, *([p] * (2 * NG_B + 2)), *params)


def _softplus(x):
    return jnp.maximum(x, 0.0) + jnp.log1p(jnp.exp(-jnp.abs(x)))


def _ssd_step_kernel(*refs, ls):
    it = iter(refs)
    z_refs = [next(it) for _ in range(NG_B)]
    xs_refs = [next(it) for _ in range(NG_B)]
    b_ref, c_ref, dt_ref = next(it), next(it), next(it)
    cw_ref, cb_ref, dtb_ref, alog_ref, dexp_ref, nw_ref, e_ref, s0_ref, conv0_ref = (next(it) for _ in range(9))
    y_ref, sout_ref = refs[-2:]
    nt = (((1,), (1,)), ((), ()))

    def conv_silu(u, off):
        w = u.shape[1]
        r8 = lax.broadcasted_iota(jnp.int32, (ls, w), 0)
        pv = conv0_ref[:, off:off + w]
        cw = cw_ref[:, off:off + w]
        acc = u * cw[CONV_W - 1:CONV_W] + cb_ref[:, off:off + w]
        for k in range(1, CONV_W):
            sh = jnp.where(r8 < k, pltpu.roll(pv, k, axis=0), pltpu.roll(u, k, axis=0))
            acc = acc + sh * cw[CONV_W - 1 - k:CONV_W - k]
        return _silu(acc)

    row = lax.broadcasted_iota(jnp.int32, (ls, LANES), 0)
    lane = lax.broadcasted_iota(jnp.int32, (ls, LANES), 1)
    dt = _softplus(dt_ref[...] + dtb_ref[...])
    acum = dt * (-jnp.exp(alog_ref[...]))
    s = 1
    while s < ls:
        acum = acum + jnp.where(row >= s, pltpu.roll(acum, s, axis=0), 0.0)
        s *= 2
    a_last = acum[ls - 1:ls, :]
    ea = jnp.exp(acum)
    de = jnp.exp(a_last - acum) * dt
    cd = jnp.exp(a_last)

    bc = conv_silu(b_ref[...], D_INNER)
    cc = conv_silu(c_ref[...], D_INNER + NG_B * N_B)
    xs = jnp.concatenate([conv_silu(xs_refs[g][...], g * GW_B) for g in range(NG_B)], axis=1)
    bgs = [bc[:, g * N_B:(g + 1) * N_B] for g in range(NG_B)]
    cgs = [cc[:, g * N_B:(g + 1) * N_B].astype(BF16) for g in range(NG_B)]
    cbs = [lax.dot_general(cgs[g], bgs[g].astype(BF16), nt, preferred_element_type=F32) for g in range(NG_B)]

    pieces = []
    for s in range(ls):
        cbx = jnp.broadcast_to(cbs[NG_B - 1][:, s:s + 1], (ls, LANES))
        for g in range(NG_B - 2, -1, -1):
            cbx = jnp.where(lane < (g + 1) * HPG_B, cbs[g][:, s:s + 1], cbx)
        dec = jnp.exp(jnp.where(row >= s, acum - acum[s:s + 1, :], NEG_INF))
        pieces.append(cbx * dec * dt[s:s + 1, :])

    def hi_lo(v):
        hi = v.astype(BF16).astype(F32)
        return [hi, v - hi]

    spread = jnp.dot(jnp.concatenate(pieces + hi_lo(ea) + hi_lo(de), axis=0).astype(BF16), e_ref[...],
                     preferred_element_type=F32)
    yd = spread[0:ls] * xs[0:1, :]
    for s in range(1, ls):
        yd = yd + spread[s * ls:(s + 1) * ls] * xs[s:s + 1, :]
    ea_x = spread[ls * ls:ls * ls + ls] + spread[ls * ls + ls:ls * ls + 2 * ls]
    de_x = spread[ls * ls + 2 * ls:ls * ls + 3 * ls] + spread[ls * ls + 3 * ls:ls * ls + 4 * ls]

    yo = jnp.concatenate(
        [lax.dot_general(cgs[g], s0_ref[g * HPG_B:(g + 1) * HPG_B].reshape(GW_B, N_B).astype(BF16), nt,
                         preferred_element_type=F32) for g in range(NG_B)], axis=1)
    y = yd + yo * ea_x + dexp_ref[...] * xs
    xt = xs * de_x
    zero_rows = SSD_CHUNK - ls
    for g in range(NG_B):
        gs = slice(g * GW_B, (g + 1) * GW_B)
        gt = y[:, gs] * _silu(z_refs[g][...])
        y_ref[:, gs] = gt * lax.rsqrt(jnp.mean(gt * gt, axis=-1, keepdims=True) + RMS_EPS) * nw_ref[:, gs]
        xt_t = jnp.concatenate([xt[:, gs], jnp.zeros((zero_rows, GW_B), F32)], axis=0).T.astype(BF16)
        bg = jnp.concatenate([bgs[g], jnp.zeros((zero_rows, N_B), F32)], axis=0).astype(BF16)
        cs = jnp.dot(xt_t, bg, preferred_element_type=F32)
        for hh in range(HPG_B):
            h = g * HPG_B + hh
            sout_ref[h] = cd[:, h:h + 1] * s0_ref[h] + cs[hh * P_B:(hh + 1) * P_B]


def _ssd_step(p, conv_w, conv_b, dt_bias, a_log, d_skip, norm_w, init_state, init_conv, layer, prev_out, *, row0,
              nseq, ls):
    assert ls == SUBLANES and row0 % ls == 0
    r0 = row0 // ls

    def blk(w, start):
        return pl.BlockSpec((ls, w), lambda b: (r0 + b, start // w))

    lane_pad = lambda v: jnp.pad(v.astype(F32), (0, LANES - v.shape[0])).reshape(1, LANES)
    spread = (jnp.arange(LANES)[:, None] == jnp.arange(D_INNER)[None, :] // P_B).astype(BF16)
    params = [conv_w.astype(F32), conv_b.astype(F32).reshape(1, CONV_DIM), lane_pad(dt_bias), lane_pad(a_log),
              jnp.repeat(d_skip.astype(F32), P_B).reshape(1, D_INNER), norm_w.astype(F32).reshape(1, D_INNER), spread]
    state_spec = pl.BlockSpec((None, None, NH_B, P_B, N_B), lambda b: (layer, b, 0, 0, 0))
    in_specs = ([blk(GW_B, ABP_Z + g * GW_B) for g in range(NG_B)]
                + [blk(GW_B, ABP_XS + g * GW_B) for g in range(NG_B)]
                + [blk(NG_B * N_B, ABP_B), blk(NG_B * N_B, ABP_C), blk(LANES, ABP_DT)]
                + [_resident(a) for a in params]
                + [state_spec, pl.BlockSpec((None, SUBLANES, CONV_DIM), lambda b: (b, 0, 0))])
    args = [p] * (2 * NG_B + 3) + params + [
        init_state, jnp.pad(init_conv.astype(F32), ((0, 0), (SUBLANES - (CONV_W - 1), 0), (0, 0)))]
    aliases = {}
    if prev_out is not None:
        aliases = {len(args): 1}
        in_specs.append(pl.BlockSpec(memory_space=pl.ANY))
        args.append(prev_out)
    return pl.pallas_call(
        functools.partial(_ssd_step_kernel, ls=ls),
        out_shape=[jax.ShapeDtypeStruct((nseq * ls, D_INNER), F32), jax.ShapeDtypeStruct(init_state.shape, F32)],
        grid=(nseq,),
        in_specs=in_specs,
        out_specs=[pl.BlockSpec((ls, D_INNER), lambda b: (b, 0)), state_spec],
        input_output_aliases=aliases,
        compiler_params=_cparams(1),
        name="ssd_step",
    )(*args)


def _two_source(np_tiles, body, p_refs, s_refs):
    i = pl.program_id(0)

    @pl.when(i < np_tiles)
    def _():
        body(*[r[...] for r in p_refs])

    @pl.when(i >= np_tiles)
    def _():
        body(*[r[...] for r in s_refs])


def _ab_out_kernel(ap_ref, yp_ref, as_ref, ys_ref, wa_ref, wy_ref, x_ref, o_ref, *, np_tiles):
    def body(att, y):
        o_ref[...] = (x_ref[...] + jnp.dot(att.astype(BF16), wa_ref[...], preferred_element_type=F32)
                      + jnp.dot(y.astype(BF16), wy_ref[...], preferred_element_type=F32))

    _two_source(np_tiles, body, (ap_ref, yp_ref), (as_ref, ys_ref))


def _c_out_kernel(*refs, np_tiles):
    p_refs, s_refs = refs[0:6], refs[6:12]
    w_ref, x_ref, o_ref = refs[12:]

    def body(o1, o2, o3, l1, l2, l3):
        l1, l2, l3 = (l if l.shape[1] == C_OUT else _unpack_heads(l, HPG_C) for l in (l1, l2, l3))
        m = jnp.maximum(jnp.maximum(l1, l2), l3)
        e1, e2, e3 = jnp.exp(l1 - m), jnp.exp(l2 - m), jnp.exp(l3 - m)
        o = (e1 * o1.astype(F32) + e2 * o2.astype(F32) + e3 * o3.astype(F32)) * (1.0 / (e1 + e2 + e3))
        o_ref[...] = x_ref[...] + jnp.dot(o.astype(BF16), w_ref[...], preferred_element_type=F32)

    _two_source(np_tiles, body, p_refs, s_refs)


def _out_proj(kern, prompt_arrs, sample_arrs, weights, x, name):
    t, d = x.shape
    tp, ts = prompt_arrs[0].shape[0], sample_arrs[0].shape[0]
    assert tp + ts == t
    tm = _pick(math.gcd(tp, ts), (512, 256, 128))
    npt = tp // tm
    in_specs = ([pl.BlockSpec((tm, a.shape[1]), lambda i: (jnp.minimum(i, npt - 1), 0)) for a in prompt_arrs]
                + [pl.BlockSpec((tm, a.shape[1]), lambda i: (jnp.maximum(i - npt, 0), 0)) for a in sample_arrs]
                + [pl.BlockSpec(w.shape, lambda i: (0, 0)) for w in weights]
                + [pl.BlockSpec((tm, d), lambda i: (i, 0))])
    return pl.pallas_call(
        functools.partial(kern, np_tiles=npt),
        out_shape=jax.ShapeDtypeStruct((t, d), F32),
        grid=(t // tm,),
        in_specs=in_specs,
        out_specs=pl.BlockSpec((tm, d), lambda i: (i, 0)),
        compiler_params=_cparams(1),
        name=name,
    )(*prompt_arrs, *sample_arrs, *weights, x)


def kernel(x_prompt, x_sample, cache_a_k, cache_a_v, state_b_ssm, state_b_conv, cache_c1_k, cache_c1_v,
           cache_c2_k, cache_c2_v, cache_c3_k, cache_c3_v, rel_bias, norm_ff1, norm_mix, norm_ff2,
           norm_final, ff1_gate, ff1_up, ff1_down, ff2_gate, ff2_up, ff2_down, ab_w_in, ab_w_out, a_sinks,
           b_conv_w, b_conv_b, b_dt_bias, b_a_log, b_d, b_norm_w, c_w_in, c_w_out):
    bp, lp, d = x_prompt.shape
    bs, ls, _ = x_sample.shape
    tp, ts = bp * lp, bs * ls
    c_cache = ((cache_c1_k, cache_c1_v), (cache_c2_k, cache_c2_v), (cache_c3_k, cache_c3_v))
    x = jnp.concatenate([x_prompt.reshape(tp, d), x_sample.reshape(ts, d)], axis=0)

    def tail_rows(p, keep, c0, w):
        return jnp.stack([lax.slice(p, (s * lp + lp - keep, c0), ((s + 1) * lp, c0 + w)) for s in range(bp)])

    a_bias_p = _band_bias(rel_bias[:, :HQ_A], 1, WIN_A)
    ab_p, ab_s, c_p = [], [], []
    ssm_s = None
    c_rows = [[c.reshape(c.shape[0], bs, c.shape[2] * HPG_C, DH_C) for c in kv] for kv in c_cache]
    c_new = [None] * len(C_GROUPS)
    for layer in range(DEPTH):
        x = _ffn(x, norm_ff1[layer], ff1_gate[layer].astype(BF16), ff1_up[layer].astype(BF16),
                 ff1_down[layer].astype(BF16))
        if layer % 2 == 0:
            e = layer // 2
            w = ab_w_in[e]
            xbc0 = AB_Q + 2 * AB_KV + D_INNER
            w_in = jnp.concatenate([w[:, :AB_Q + 2 * AB_KV], w[:, xbc0 + CONV_DIM:],
                                    jnp.zeros((d, ABP_Z - ABP_DT - NH_B), w.dtype),
                                    w[:, AB_Q + 2 * AB_KV:xbc0], w[:, xbc0:xbc0 + CONV_DIM]], axis=1).astype(BF16)
            p = _rms_matmul(x, norm_mix[layer], w_in)
            ssd_params = (b_conv_w[e], b_conv_b[e], b_dt_bias[e], b_a_log[e], b_d[e], b_norm_w[e])

            att_p = _band_attn(p, a_bias_p, a_sinks[e], nseq=bp, seqlen=lp, hq=HQ_A, hkv=HKV_A, dh=DH_A,
                               q_start=ABP_Q, k_start=ABP_K, v_start=ABP_V, emit_lse=False, out_dtype=BF16)
            y_p, ssm_p = _ssd(p, *ssd_params, nseq=bp, seqlen=lp, out_dtype=BF16)

            lbuf = cache_a_k.shape[2]
            bias_c, bias_n = _step_bias(rel_bias[:, :HQ_A], lbuf, ls, 1, WIN_A, head_rows=False)
            sink_rows = jnp.repeat(a_sinks[e].astype(F32), ls).reshape(HQ_A * ls, 1)
            att_s, k_s, v_s = _step_attn(p, cache_a_k[e].reshape(bs, lbuf, AB_KV), cache_a_v[e].reshape(bs, lbuf, AB_KV),
                                         bias_c, bias_n, sink_rows, row0=tp, ls=ls, hq=HQ_A, hkv=HKV_A, dh=DH_A,
                                         q_start=ABP_Q, k_start=ABP_K, v_start=ABP_V, emit_lse=False)
            y_s, ssm_s = _ssd_step(p, *ssd_params, state_b_ssm, state_b_conv[e], e, ssm_s, row0=tp, nseq=bs, ls=ls)

            w_out = ab_w_out[e].astype(BF16)
            x = _out_proj(_ab_out_kernel, (att_p, y_p), (att_s, y_s), (w_out[:AB_Q], w_out[AB_Q:]), x, "ab_out")

            keep = min(WIN_A, lp)
            xbc_p = tail_rows(p, min(CONV_W - 1, lp), ABP_XS, CONV_DIM)
            xbc_s = p[tp:, ABP_XS:].reshape(bs, ls, CONV_DIM)
            ab_p.append((tail_rows(p, keep, ABP_K, AB_KV).reshape(bp, keep, HKV_A, DH_A),
                         tail_rows(p, keep, ABP_V, AB_KV).reshape(bp, keep, HKV_A, DH_A),
                         ssm_p,
                         jnp.concatenate([jnp.zeros((bp, CONV_W - 1, CONV_DIM), F32), xbc_p],
                                         axis=1)[:, -(CONV_W - 1):]))
            ab_s.append((k_s.reshape(bs, lbuf, HKV_A, DH_A), v_s.reshape(bs, lbuf, HKV_A, DH_A),
                         jnp.concatenate([state_b_conv[e], xbc_s], axis=1)[:, -(CONV_W - 1):]))
        else:
            o = layer // 2
            p = _rms_matmul(x, norm_mix[layer], c_w_in[o].astype(BF16))
            outs_p, lses_p, outs_s, lses_s, st_p = [], [], [], [], []
            for gi, (win, dil) in enumerate(C_GROUPS):
                n_win = win // dil
                bias_heads = rel_bias[:, gi * HPG_C:(gi + 1) * HPG_C]
                starts = dict(q_start=gi * C_OUT, k_start=(3 + gi) * C_OUT, v_start=(6 + gi) * C_OUT)
                if dil == 1:
                    o_p, l_p = _band_attn(p, _band_bias(bias_heads, dil, n_win), None, nseq=bp, seqlen=lp,
                                          hq=HPG_C, hkv=HPG_C, dh=DH_C, emit_lse=True, out_dtype=BF16, **starts)
                else:
                    o_p, l_p = _dil_attn(p, _band_bias(bias_heads, dil, n_win), nseq=bp, seqlen=lp, dil=dil,
                                         nh=HPG_C, dh=DH_C, **starts)
                lbuf = c_cache[gi][0].shape[2]
                bias_c, bias_n = _step_bias(bias_heads, lbuf, ls, dil, n_win, head_rows=True)
                o_s, l_s, k_s, v_s = _step_attn_rows(p, c_rows[gi][0], c_rows[gi][1], o, c_new[gi], bias_c, bias_n,
                                                     row0=tp, ls=ls, nh=HPG_C, dh=DH_C, **starts)
                c_new[gi] = (k_s, v_s)
                outs_p.append(o_p)
                lses_p.append(l_p)
                outs_s.append(o_s)
                lses_s.append(l_s)
                keep = min(win, lp)
                st_p += [tail_rows(p, keep, starts["k_start"], C_OUT).reshape(bp, keep, HPG_C, DH_C),
                         tail_rows(p, keep, starts["v_start"], C_OUT).reshape(bp, keep, HPG_C, DH_C)]
            x = _out_proj(_c_out_kernel, (*outs_p, *lses_p), (*outs_s, *lses_s), (c_w_out[o].astype(BF16),), x,
                          "c_out")
            c_p.append(tuple(st_p))
        x = _ffn(x, norm_ff2[layer], ff2_gate[layer].astype(BF16), ff2_up[layer].astype(BF16),
                 ff2_down[layer].astype(BF16), final_w=norm_final if layer == DEPTH - 1 else None)

    def stack(states):
        return [jnp.stack([s[i] for s in states]) for i in range(len(states[0]))]

    a_k_s, a_v_s, conv_s = stack(ab_s)
    c_s = [a.reshape(c.shape) for kv_new, kv in zip(c_new, c_cache) for a, c in zip(kv_new, kv)]
    return (x[:tp].reshape(bp, lp, d), x[tp:].reshape(bs, ls, d),
            *stack(ab_p), *stack(c_p), a_k_s, a_v_s, ssm_s, conv_s, *c_s)
```
